```python
import math
import jax, jax.numpy as jnp
from jax import lax
import numpy as np

D_MODEL = 1024
BATCH = 4
SEQ = 8192
DEPTH = 1

CHUNK = 64
N_MEM = 256
EPS = 1e-6
SB_HEADS = 8
SB_HEAD_DIM = 64
SB_WIDTH = SB_HEADS * SB_HEAD_DIM
SB_Q_BLOCK = 128
GDN_HEADS = 4
GDN_HEAD_DIM = 128
GDN_WIDTH = GDN_HEADS * GDN_HEAD_DIM
CONV_WIDTH = 4
X_HEADS = 4
X_HEAD_DIM = D_MODEL // X_HEADS
PEER_HEADS = 8
PEER_KEYS = 128
PEER_EXPERTS = PEER_KEYS * PEER_KEYS
PEER_TOPK = 16
PEER_QDIM = 256
PEER_TOKEN_BLOCK = 128
IN_SPLIT_SIZES = (SB_WIDTH, SB_WIDTH, SB_WIDTH, GDN_WIDTH, GDN_WIDTH, GDN_WIDTH, GDN_WIDTH,
                  GDN_HEADS, GDN_HEADS, D_MODEL, D_MODEL)
IN_COLS = 3 * SB_WIDTH + 4 * GDN_WIDTH + 2 * GDN_HEADS + 2 * D_MODEL

kernel_name = "hybrid_stickbreak_gdn_peer_block"


def rmsnorm(x, g):
    xf = x.astype(jnp.float32)
    y = xf * lax.rsqrt(jnp.mean(xf * xf, axis=-1, keepdims=True) + EPS)
    return (y * g.astype(jnp.float32)).astype(x.dtype)


def l2norm(x):
    xf = x.astype(jnp.float32)
    return xf * lax.rsqrt(jnp.sum(xf * xf, axis=-1, keepdims=True) + EPS)


def causal_dwconv(x, w):
    K = w.shape[0]
    S = x.shape[1]
    xp = jnp.pad(x, ((0, 0), (K - 1, 0), (0, 0)))
    return sum(xp[:, j:j + S] * w[j] for j in range(K))


def stick_breaking_attention(q, k, v):
    S, d = q.shape[2], q.shape[3]
    scale = d ** -0.5
    outs = []
    for i in range(S // SB_Q_BLOCK):
        q0 = i * SB_Q_BLOCK
        kv_len = q0 + SB_Q_BLOCK
        qb = q[:, :, q0:kv_len]
        kb = k[:, :, :kv_len]
        vb = v[:, :, :kv_len]
        z = jnp.einsum('bhqd,bhkd->bhqk', qb, kb).astype(jnp.float32) * scale
        qpos = q0 + jnp.arange(SB_Q_BLOCK)[:, None]
        kpos = jnp.arange(kv_len)[None, :]
        mask = kpos < qpos
        log_keep = jnp.where(mask, jax.nn.log_sigmoid(-z), 0.0)
        after = lax.cumsum(log_keep, axis=3, reverse=True) - log_keep
        a = jnp.where(mask, jnp.exp(jax.nn.log_sigmoid(z) + after), 0.0)
        outs.append(jnp.einsum('bhqk,bhkd->bhqd', a.astype(v.dtype), vb))
    return jnp.concatenate(outs, axis=2)


def gated_delta_rule(q, k, v, beta, g):
    B, S, H, dk = q.shape
    dv = v.shape[-1]
    N, C = S // CHUNK, CHUNK
    f32 = jnp.float32

    def chunk4(t):
        return jnp.moveaxis(t.astype(f32).reshape(B, N, C, H, t.shape[-1]), 3, 1)

    def chunk3(t):
        return jnp.moveaxis(t.astype(f32).reshape(B, N, C, H), 3, 1)

    q = chunk4(q) * (dk ** -0.5)
    k = chunk4(k)
    v = chunk4(v)
    beta = chunk3(beta)
    gc = jnp.cumsum(chunk3(g), axis=-1)
    idx = jnp.arange(C)
    incl = idx[:, None] >= idx[None, :]
    strict = idx[:, None] > idx[None, :]
    decay = jnp.exp(jnp.where(incl, gc[..., :, None] - gc[..., None, :], -jnp.inf))
    kb = k * beta[..., None]
    vb = v * beta[..., None]
    a = jnp.where(strict, jnp.einsum('bhncd,bhnmd->bhncm', kb, k) * decay, 0.0)
    eye = jnp.eye(C, dtype=f32)
    rhs = jnp.concatenate([vb, kb * jnp.exp(gc)[..., None]], axis=-1)
    sol = lax.linalg.triangular_solve(a + eye, rhs, left_side=True, lower=True, unit_diagonal=True)
    u = sol[..., :dv]
    w = sol[..., dv:]
    qk = jnp.einsum('bhncd,bhnmd->bhncm', q, k) * decay
    q_dec = q * jnp.exp(gc)[..., None]
    k_dec = k * jnp.exp(gc[..., -1:] - gc)[..., None]
    chunk_decay = jnp.exp(gc[..., -1])
    xs = tuple(jnp.moveaxis(t, 2, 0) for t in (q_dec, k_dec, u, w, qk, chunk_decay))

    def step(state, inp):
        qd, kd, u_n, w_n, qk_n, cd = inp
        v_new = u_n - jnp.einsum('bhcd,bhde->bhce', w_n, state)
        o = jnp.einsum('bhcd,bhde->bhce', qd, state) + jnp.einsum('bhcm,bhme->bhce', qk_n, v_new)
        state = state * cd[..., None, None] + jnp.einsum('bhcd,bhce->bhde', kd, v_new)
        return state, o

    state0 = jnp.zeros((B, H, dk, dv), f32)
    _, o = lax.scan(step, state0, xs)
    return jnp.transpose(o, (1, 0, 3, 2, 4)).reshape(B, S, H, dv)


def token_mixer(xn, w_in, gdn_conv, gdn_a_log, gdn_dt_bias, gdn_norm_g, w_sb_up, w_gdn_up, w_mix_out):
    B, S, _ = xn.shape
    proj = xn @ w_in
    offs = [int(o) for o in np.cumsum(IN_SPLIT_SIZES)[:-1]]
    (sb_q, sb_k, sb_v, g_q, g_k, g_v, g_z, g_b, g_a, gate_sb, gate_gdn) = jnp.split(proj, offs, axis=-1)

    def heads_sb(t):
        return t.reshape(B, S, SB_HEADS, SB_HEAD_DIM).transpose(0, 2, 1, 3)
    o_sb = stick_breaking_attention(heads_sb(sb_q), heads_sb(sb_k), heads_sb(sb_v))
    o_sb = o_sb.transpose(0, 2, 1, 3).reshape(B, S, SB_WIDTH)

    qkv = jax.nn.silu(causal_dwconv(jnp.concatenate([g_q, g_k, g_v], axis=-1), gdn_conv))
    cq, ck, cv = jnp.split(qkv, 3, axis=-1)
    cq = l2norm(cq.reshape(B, S, GDN_HEADS, GDN_HEAD_DIM))
    ck = l2norm(ck.reshape(B, S, GDN_HEADS, GDN_HEAD_DIM))
    cv = cv.reshape(B, S, GDN_HEADS, GDN_HEAD_DIM)
    beta = jax.nn.sigmoid(g_b.astype(jnp.float32))
    g = -jnp.exp(gdn_a_log.astype(jnp.float32)) * jax.nn.softplus(g_a.astype(jnp.float32) + gdn_dt_bias.astype(jnp.float32))
    o_g = gated_delta_rule(cq, ck, cv, beta, g).astype(xn.dtype)
    o_g = rmsnorm(o_g, gdn_norm_g) * jax.nn.silu(g_z.reshape(B, S, GDN_HEADS, GDN_HEAD_DIM))
    o_g = o_g.reshape(B, S, GDN_WIDTH)

    merged = jax.nn.sigmoid(gate_sb) * (o_sb @ w_sb_up) + jax.nn.sigmoid(gate_gdn) * (o_g @ w_gdn_up)
    return merged @ w_mix_out


def memory_cross_attention(hn, mn, w_cq, w_ckv, w_co):
    B, S, _ = hn.shape
    M = mn.shape[1]
    q = (hn @ w_cq).reshape(B, S, X_HEADS, X_HEAD_DIM)
    k, v = jnp.split(mn @ w_ckv, 2, axis=-1)
    k = k.reshape(B, M, X_HEADS, X_HEAD_DIM)
    v = v.reshape(B, M, X_HEADS, X_HEAD_DIM)
    s = jnp.einsum('bshd,bmhd->bhsm', q, k).astype(jnp.float32) * (X_HEAD_DIM ** -0.5)
    p = jax.nn.softmax(s, axis=-1).astype(v.dtype)
    o = jnp.einsum('bhsm,bmhd->bshd', p, v).reshape(B, S, D_MODEL)
    return o @ w_co


def peer(xn, w_pq, subkeys, peer_u, peer_v):
    B, S, D = xn.shape
    T = B * S
    xt = xn.reshape(T, D)
    q = (xt @ w_pq).reshape(T, PEER_HEADS, 2, PEER_QDIM // 2)
    s = jnp.einsum('thpd,hpnd->thpn', q, subkeys).astype(jnp.float32)
    top_s, top_i = lax.top_k(s, PEER_TOPK)
    cand_s = top_s[:, :, 0, :, None] + top_s[:, :, 1, None, :]
    cand_i = top_i[:, :, 0, :, None] * PEER_KEYS + top_i[:, :, 1, None, :]
    kk = PEER_TOPK * PEER_TOPK
    best_s, best_pos = lax.top_k(cand_s.reshape(T, PEER_HEADS, kk), PEER_TOPK)
    expert = jnp.take_along_axis(cand_i.reshape(T, PEER_HEADS, kk), best_pos, axis=-1)
    gate = jax.nn.softmax(best_s, axis=-1).astype(xn.dtype)
    nb = T // PEER_TOKEN_BLOCK

    def block(args):
        xb, eb, gb = args
        act = jax.nn.gelu(jnp.einsum('thkd,td->thk', peer_u[eb], xb), approximate=False) * gb
        return jnp.einsum('thk,thkd->td', act, peer_v[eb])

    out = lax.map(block, (xt.reshape(nb, PEER_TOKEN_BLOCK, D),
                          expert.reshape(nb, PEER_TOKEN_BLOCK, PEER_HEADS, PEER_TOPK),
                          gate.reshape(nb, PEER_TOKEN_BLOCK, PEER_HEADS, PEER_TOPK)))
    return out.reshape(B, S, D)


def setup_inputs(seed: int = 0) -> dict:
    key = jax.random.key(seed)
    ks = jax.random.split(key, 24)
    f32 = jnp.float32

    def nrm(k, shape, scale):
        return jax.random.normal(k, shape, f32) * scale

    def gain(k, shape):
        return 1.0 + 0.02 * jax.random.normal(k, shape, f32)

    L, D = DEPTH, D_MODEL
    a_log = jnp.log(jax.random.uniform(ks[5], (L, GDN_HEADS), f32, 1.0, 16.0))
    dt = jnp.exp(jax.random.uniform(ks[6], (L, GDN_HEADS), f32, math.log(1e-3), math.log(1e-1)))
    dt_bias = dt + jnp.log(-jnp.expm1(-dt))
    return {
        "x": nrm(ks[0], (BATCH, SEQ, D), 1.0),
        "mem": nrm(ks[1], (BATCH, N_MEM, D), 1.0),
        "g_mix": gain(ks[2], (L, D)),
        "w_in": nrm(ks[3], (L, D, IN_COLS), D ** -0.5),
        "gdn_conv": nrm(ks[4], (L, CONV_WIDTH, 3 * GDN_WIDTH), CONV_WIDTH ** -0.5),
        "gdn_a_log": a_log,
        "gdn_dt_bias": dt_bias,
        "gdn_norm_g": gain(ks[7], (L, GDN_HEAD_DIM)),
        "w_sb_up": nrm(ks[8], (L, SB_WIDTH, D), SB_WIDTH ** -0.5),
        "w_gdn_up": nrm(ks[9], (L, GDN_WIDTH, D), GDN_WIDTH ** -0.5),
        "w_mix_out": nrm(ks[10], (L, D, D), D ** -0.5),
        "g_cross": gain(ks[11], (L, D)),
        "g_mem": gain(ks[12], (L, D)),
        "w_cq": nrm(ks[13], (L, D, D), D ** -0.5),
        "w_ckv": nrm(ks[14], (L, D, 2 * D), D ** -0.5),
        "w_co": nrm(ks[15], (L, D, D), D ** -0.5),
        "g_peer": gain(ks[16], (L, D)),
        "w_pq": nrm(ks[17], (L, D, PEER_HEADS * PEER_QDIM), D ** -0.5),
        "peer_subkeys": nrm(ks[18], (L, PEER_HEADS, 2, PEER_KEYS, PEER_QDIM // 2), (PEER_QDIM // 2) ** -0.5),
        "peer_u": nrm(ks[19], (L, PEER_EXPERTS, D), D ** -0.5),
        "peer_v": nrm(ks[20], (L, PEER_EXPERTS, D), PEER_HEADS ** -0.5),
        "g_final": gain(ks[21], (D,)),
    }


def reference(x, mem, g_mix, w_in, gdn_conv, gdn_a_log, gdn_dt_bias, gdn_norm_g, w_sb_up, w_gdn_up,
              w_mix_out, g_cross, g_mem, w_cq, w_ckv, w_co, g_peer, w_pq, peer_subkeys, peer_u, peer_v,
              g_final):
    for l in range(DEPTH):
        x = x + token_mixer(rmsnorm(x, g_mix[l]), w_in[l], gdn_conv[l], gdn_a_log[l], gdn_dt_bias[l],
                            gdn_norm_g[l], w_sb_up[l], w_gdn_up[l], w_mix_out[l])
        x = x + memory_cross_attention(rmsnorm(x, g_cross[l]), rmsnorm(mem, g_mem[l]),
                                       w_cq[l], w_ckv[l], w_co[l])
        x = x + peer(rmsnorm(x, g_peer[l]), w_pq[l], peer_subkeys[l], peer_u[l], peer_v[l])
    return rmsnorm(x, g_final)
```

```python
import functools

import jax
import jax.numpy as jnp
from jax import lax
from jax.experimental import pallas as pl
from jax.experimental.pallas import tpu as pltpu

F32 = jnp.float32
BF16 = jnp.bfloat16

EPS = 1e-6
SB_HEADS = 8
SB_HEAD_DIM = 64
GDN_HEADS = 4
GDN_HEAD_DIM = 128
CONV_WIDTH = 4
CHUNK = 64
X_HEADS = 4
PEER_HEADS = 8
PEER_KEYS = 128
PEER_TOPK = 16

LANES = 128
VMEM_LIMIT = 56 * 1024 * 1024
SB_SKIP_LOG = -104.0


def _dot(a, b):
    return jnp.dot(a, b, preferred_element_type=F32)


def _dot_nt(a, b):
    return lax.dot_general(a, b, (((1,), (1,)), ((), ())), preferred_element_type=F32)


def _dot_tn(a, b):
    return lax.dot_general(a, b, (((0,), (0,)), ((), ())), preferred_element_type=F32)


def _split3(a):
    hi = a.astype(BF16)
    r1 = a - hi.astype(F32)
    mid = r1.astype(BF16)
    lo = (r1 - mid.astype(F32)).astype(BF16)
    return hi, mid, lo


def _dot_exact_rhs(a, m_bf16):
    hi, mid, lo = _split3(a)
    return _dot(hi, m_bf16) + _dot(mid, m_bf16) + _dot(lo, m_bf16)


def _dot_exact_lhs(m_bf16, b):
    hi, mid, lo = _split3(b)
    return _dot(m_bf16, hi) + _dot(m_bf16, mid) + _dot(m_bf16, lo)


def _dot_hp(a, b):
    ah = a.astype(BF16)
    al = (a - ah.astype(F32)).astype(BF16)
    bh = b.astype(BF16)
    bl = (b - bh.astype(F32)).astype(BF16)
    return _dot(ah, bh) + _dot(ah, bl) + _dot(al, bh)


def _rms(x, g):
    return x * lax.rsqrt(jnp.mean(x * x, axis=-1, keepdims=True) + EPS) * g


def _softplus(x):
    return jnp.maximum(x, 0.0) + jnp.log1p(jnp.exp(-jnp.abs(x)))


def _sigmoid(x):
    return 1.0 / (1.0 + jnp.exp(-x))


def _params(sem):
    return pltpu.CompilerParams(dimension_semantics=sem, vmem_limit_bytes=VMEM_LIMIT)


def _kv_kernel(mem_ref, g_ref, w_ref, o_ref):
    mn = _rms(mem_ref[...], g_ref[...]).astype(BF16)
    o_ref[...] = _dot(mn, w_ref[...]).astype(BF16)


def _kv_call(mem2d, g_mem, w_ckv, n_mem):
    rows, d = mem2d.shape
    return pl.pallas_call(
        _kv_kernel,
        grid=(rows // n_mem,),
        in_specs=[pl.BlockSpec((n_mem, d), lambda i: (i, 0)),
                  pl.BlockSpec((1, d), lambda i: (0, 0)),
                  pl.BlockSpec((d, 2 * d), lambda i: (0, 0))],
        out_specs=pl.BlockSpec((n_mem, 2 * d), lambda i: (i, 0)),
        out_shape=jax.ShapeDtypeStruct((rows, 2 * d), BF16),
        compiler_params=_params(("arbitrary",)),
        name="kv",
    )(mem2d, g_mem, w_ckv)


SB_W = SB_HEADS * SB_HEAD_DIM
GDN_W = GDN_HEADS * GDN_HEAD_DIM
C_SB = (0, 3 * SB_W)
C_GQ = (C_SB[1], C_SB[1] + 3 * GDN_W)
C_GZ = (C_GQ[1], C_GQ[1] + GDN_W)


def _inproj_kernel(x_ref, g_ref, w_ref, sb_ref, gq_ref, gz_ref, gt_ref, ba_ref, *, d):
    xn = _rms(x_ref[...], g_ref[...]).astype(BF16)
    c_gt = (C_GZ[1], C_GZ[1] + 2 * d)
    c_ba = (c_gt[1], c_gt[1] + LANES)

    def mm(c):
        return _dot(xn, w_ref[:, c[0]:c[1]])

    sb_ref[...] = mm(C_SB).astype(BF16)
    gq_ref[...] = mm(C_GQ)
    gz_ref[...] = mm(C_GZ)
    gt_ref[...] = mm(c_gt).astype(BF16)
    ba_ref[...] = mm(c_ba)


def _inproj_call(x2d, g_mix, w_p, tm):
    t, d = x2d.shape
    n = w_p.shape[1]
    row = lambda i: (i, 0)
    fixed = lambda i: (0, 0)
    return pl.pallas_call(
        functools.partial(_inproj_kernel, d=d),
        grid=(t // tm,),
        in_specs=[pl.BlockSpec((tm, d), row),
                  pl.BlockSpec((1, d), fixed),
                  pl.BlockSpec((d, n), fixed)],
        out_specs=[pl.BlockSpec((tm, 3 * SB_W), row),
                   pl.BlockSpec((tm, 3 * GDN_W), row),
                   pl.BlockSpec((tm, GDN_W), row),
                   pl.BlockSpec((tm, 2 * d), row),
                   pl.BlockSpec((tm, LANES), row)],
        out_shape=[jax.ShapeDtypeStruct((t, 3 * SB_W), BF16),
                   jax.ShapeDtypeStruct((t, 3 * GDN_W), F32),
                   jax.ShapeDtypeStruct((t, GDN_W), F32),
                   jax.ShapeDtypeStruct((t, 2 * d), BF16),
                   jax.ShapeDtypeStruct((t, LANES), F32)],
        compiler_params=_params(("arbitrary",)),
        name="inproj",
    )(x2d, g_mix, w_p)


SB_BLK = 128


def _sb_kernel(q_ref, k_ref, v_ref, o_ref, acc_ref, c_ref, *, seq):
    blk = SB_BLK
    scale = SB_HEAD_DIM ** -0.5
    lane = lax.broadcasted_iota(jnp.int32, (blk, LANES), 1)
    head0 = lane < SB_HEAD_DIM
    r_i = lax.broadcasted_iota(jnp.int32, (blk, blk), 0)
    c_i = lax.broadcasted_iota(jnp.int32, (blk, blk), 1)
    causal = c_i < r_i
    jj = lax.broadcasted_iota(jnp.int32, (blk, 2 * blk), 0)
    ss = lax.broadcasted_iota(jnp.int32, (blk, 2 * blk), 1)
    m_ext = jnp.where(ss >= blk, 1.0, jnp.where(jj > ss, 1.0, 0.0)).astype(BF16)

    def tile(qm, kblk, vblk, hd, masked):
        z = _dot_nt(qm, kblk) * scale
        sp = _softplus(z)
        lk = -sp
        if masked:
            lk = jnp.where(causal, lk, 0.0)
        ext = _dot_exact_rhs(lk, m_ext)
        after = ext[:, :blk]
        tot = ext[:, blk:]
        c = c_ref[hd]
        a = jnp.exp((z - sp) + after + c)
        if masked:
            a = jnp.where(causal, a, 0.0)
        acc_ref[hd] += _dot(a.astype(BF16), vblk)
        c_new = c + tot
        c_ref[hd] = c_new
        return jnp.max(c_new)

    def qblock(qi, carry):
        q0 = pl.multiple_of(qi * blk, blk)
        q = q_ref[pl.ds(q0, blk), :]
        zero = jnp.zeros_like(q)
        qm0 = jnp.where(head0, q, zero)
        qm1 = jnp.where(head0, zero, q)
        acc_ref[...] = jnp.zeros_like(acc_ref)
        c_ref[...] = jnp.zeros_like(c_ref)
        kd = k_ref[pl.ds(q0, blk), :]
        vd = v_ref[pl.ds(q0, blk), :]
        m0 = tile(qm0, kd, vd, 0, True)
        m1 = tile(qm1, kd, vd, 1, True)

        def cond(st):
            kb, cm = st
            return jnp.logical_and(kb >= 0, cm > SB_SKIP_LOG)

        def body(st):
            kb, _ = st
            k0 = pl.multiple_of(kb * blk, blk)
            kblk = k_ref[pl.ds(k0, blk), :]
            vblk = v_ref[pl.ds(k0, blk), :]
            n0 = tile(qm0, kblk, vblk, 0, False)
            n1 = tile(qm1, kblk, vblk, 1, False)
            return kb - 1, jnp.maximum(n0, n1)

        lax.while_loop(cond, body, (qi - 1, jnp.maximum(m0, m1)))
        o = jnp.where(head0, acc_ref[0], acc_ref[1])
        o_ref[pl.ds(q0, blk), :] = o.astype(BF16)
        return carry

    lax.fori_loop(0, seq // blk, qblock, 0)


def _sb_call(sbqkv, batch, seq):
    t = sbqkv.shape[0]
    ng = SB_W // LANES
    return pl.pallas_call(
        functools.partial(_sb_kernel, seq=seq),
        grid=(batch, ng),
        in_specs=[pl.BlockSpec((seq, LANES), lambda b, g: (b, g)),
                  pl.BlockSpec((seq, LANES), lambda b, g: (b, ng + g)),
                  pl.BlockSpec((seq, LANES), lambda b, g: (b, 2 * ng + g))],
        out_specs=pl.BlockSpec((seq, LANES), lambda b, g: (b, g)),
        out_shape=jax.ShapeDtypeStruct((t, SB_W), BF16),
        scratch_shapes=[pltpu.VMEM((2, SB_BLK, LANES), F32),
                        pltpu.VMEM((2, SB_BLK, SB_BLK), F32)],
        compiler_params=_params(("arbitrary", "arbitrary")),
        name="sb",
    )(sbqkv, sbqkv, sbqkv)


GDN_PAD = 8


def _gdn_kernel(qkv_ref, z_ref, ba_ref, conv_ref, alog_ref, dtb_ref, ng_ref, o_ref,
                xbuf, qs, ks, vs, os_, gcs, egs, eks, ecs, bts, state, *, blk):
    hd = GDN_HEAD_DIM
    w3 = 3 * GDN_W

    @pl.when(pl.program_id(1) == 0)
    def _():
        state[...] = jnp.zeros_like(state)
        xbuf[0:GDN_PAD, :] = jnp.zeros((GDN_PAD, w3), F32)

    xbuf[GDN_PAD:GDN_PAD + blk, :] = qkv_ref[...]
    cw = conv_ref[...]
    y = xbuf[GDN_PAD:GDN_PAD + blk, :] * cw[CONV_WIDTH - 1:CONV_WIDTH, :]
    for j in range(CONV_WIDTH - 1):
        d = CONV_WIDTH - 1 - j
        y = y + xbuf[GDN_PAD - d:GDN_PAD - d + blk, :] * cw[j:j + 1, :]
    xbuf[0:GDN_PAD, :] = xbuf[blk:blk + GDN_PAD, :]
    y = y * _sigmoid(y)
    for h in range(GDN_HEADS):
        qh = y[:, h * hd:(h + 1) * hd]
        kh = y[:, GDN_W + h * hd:GDN_W + (h + 1) * hd]
        qs[:, h * hd:(h + 1) * hd] = qh * (lax.rsqrt(jnp.sum(qh * qh, axis=-1, keepdims=True) + EPS)
                                          * (hd ** -0.5))
        ks[:, h * hd:(h + 1) * hd] = kh * lax.rsqrt(jnp.sum(kh * kh, axis=-1, keepdims=True) + EPS)
    vs[...] = y[:, 2 * GDN_W:]

    ba = ba_ref[...]
    lane = lax.broadcasted_iota(jnp.int32, (blk, LANES), 1)
    g = -jnp.exp(alog_ref[...]) * _softplus(ba + dtb_ref[...])
    g = jnp.where(jnp.logical_and(lane >= GDN_HEADS, lane < 2 * GDN_HEADS), g, 0.0)
    rr = lax.broadcasted_iota(jnp.int32, (blk, blk), 0)
    cc = lax.broadcasted_iota(jnp.int32, (blk, blk), 1)
    same = (rr // CHUNK) == (cc // CHUNK)
    bd_ones = jnp.where(same, 1.0, 0.0).astype(BF16)
    bd_tril = jnp.where(same, jnp.where(cc <= rr, 1.0, 0.0), 0.0).astype(BF16)
    gc = _dot_exact_lhs(bd_tril, g)
    gl = _dot_exact_lhs(bd_ones, g)
    gcs[...] = gc
    egs[...] = jnp.exp(gc)
    eks[...] = jnp.exp(gl - gc)
    ecs[...] = jnp.exp(gl)
    bts[...] = _sigmoid(ba)

    ri = lax.broadcasted_iota(jnp.int32, (CHUNK, CHUNK), 0)
    ci = lax.broadcasted_iota(jnp.int32, (CHUNK, CHUNK), 1)
    eye = jnp.where(ri == ci, 1.0, 0.0)
    incl = ri >= ci
    strict = ri > ci

    def chunk(cidx, carry):
        r0 = pl.multiple_of(cidx * CHUNK, CHUNK)
        rows = pl.ds(r0, CHUNK)
        gct = gcs[rows, :]
        egt = egs[rows, :]
        ekt = eks[rows, :]
        ect = ecs[rows, :]
        btt = bts[rows, :]
        for h in range(GDN_HEADS):
            cols = slice(h * hd, (h + 1) * hd)
            q = qs[rows, cols]
            k = ks[rows, cols]
            v = vs[rows, cols]
            gcol = gct[:, GDN_HEADS + h:GDN_HEADS + h + 1]
            grow = jnp.sum(eye * gcol, axis=0, keepdims=True)
            dm = jnp.exp(jnp.where(incl, gcol - grow, -jnp.inf))
            bcol = btt[:, h:h + 1]
            kb = k * bcol
            vb = v * bcol
            kbf = k.astype(BF16)
            a = jnp.where(strict, _dot_nt(kb.astype(BF16), kbf) * dm, 0.0)
            n = -a
            x = eye + n
            p = _dot_hp(n, n)
            for it in range(5):
                x = x + _dot_hp(x, p)
                if it < 4:
                    p = _dot_hp(p, p)
            egc = egt[:, GDN_HEADS + h:GDN_HEADS + h + 1]
            u = _dot_hp(x, vb)
            w = _dot_hp(x, kb * egc)
            qk = _dot_nt(q.astype(BF16), kbf) * dm
            qd = q * egc
            kd = k * ekt[:, GDN_HEADS + h:GDN_HEADS + h + 1]
            cd = ect[0:1, GDN_HEADS + h:GDN_HEADS + h + 1]
            s = state[h]
            sb16 = s.astype(BF16)
            vnew = u - _dot(w.astype(BF16), sb16)
            vn16 = vnew.astype(BF16)
            os_[rows, cols] = _dot(qd.astype(BF16), sb16) + _dot(qk.astype(BF16), vn16)
            state[h] = s * cd + _dot_tn(kd.astype(BF16), vn16)
        return carry

    lax.fori_loop(0, blk // CHUNK, chunk, 0)

    zg = z_ref[...]
    ng = ng_ref[...]
    for h in range(GDN_HEADS):
        cols = slice(h * hd, (h + 1) * hd)
        oh = _rms(os_[:, cols], ng)
        zh = zg[:, cols]
        o_ref[:, cols] = (oh * (zh * _sigmoid(zh))).astype(BF16)


def _gdn_call(gqkv, gz, gba, conv_w, alog_l, dtb_l, norm_g, batch, seq, blk):
    t = gqkv.shape[0]
    nb = seq // blk
    row = lambda b, i: (b * nb + i, 0)
    fixed = lambda b, i: (0, 0)
    w3 = 3 * GDN_W
    return pl.pallas_call(
        functools.partial(_gdn_kernel, blk=blk),
        grid=(batch, nb),
        in_specs=[pl.BlockSpec((blk, w3), row),
                  pl.BlockSpec((blk, GDN_W), row),
                  pl.BlockSpec((blk, LANES), row),
                  pl.BlockSpec((CONV_WIDTH, w3), fixed),
                  pl.BlockSpec((1, LANES), fixed),
                  pl.BlockSpec((1, LANES), fixed),
                  pl.BlockSpec((1, GDN_HEAD_DIM), fixed)],
        out_specs=pl.BlockSpec((blk, GDN_W), row),
        out_shape=jax.ShapeDtypeStruct((t, GDN_W), BF16),
        scratch_shapes=[pltpu.VMEM((blk + GDN_PAD, w3), F32),
                        pltpu.VMEM((blk, GDN_W), F32),
                        pltpu.VMEM((blk, GDN_W), F32),
                        pltpu.VMEM((blk, GDN_W), F32),
                        pltpu.VMEM((blk, GDN_W), F32),
                        pltpu.VMEM((blk, LANES), F32),
                        pltpu.VMEM((blk, LANES), F32),
                        pltpu.VMEM((blk, LANES), F32),
                        pltpu.VMEM((blk, LANES), F32),
                        pltpu.VMEM((blk, LANES), F32),
                        pltpu.VMEM((GDN_HEADS, GDN_HEAD_DIM, GDN_HEAD_DIM), F32)],
        compiler_params=_params(("arbitrary", "arbitrary")),
        name="gdn",
    )(gqkv, gz, gba, conv_w, alog_l, dtb_l, norm_g)


def _merge_kernel(x_ref, osb_ref, og_ref, gt_ref, kv_ref, wsb_ref, wg_ref, wmix_ref,
                  gcr_ref, wcq_ref, wco_ref, gpe_ref, x2_ref, xp_ref, *, d):
    gates = gt_ref[...].astype(F32)
    merged = (_sigmoid(gates[:, :d]) * _dot(osb_ref[...], wsb_ref[...])
              + _sigmoid(gates[:, d:]) * _dot(og_ref[...], wg_ref[...]))
    x1 = x_ref[...] + _dot(merged.astype(BF16), wmix_ref[...])
    hn = _rms(x1, gcr_ref[...]).astype(BF16)
    q = _dot(hn, wcq_ref[...])
    xd = d // X_HEADS
    kv = kv_ref[...]
    outs = []
    for h in range(X_HEADS):
        qh = q[:, h * xd:(h + 1) * xd].astype(BF16)
        kh = kv[:, h * xd:(h + 1) * xd]
        vh = kv[:, d + h * xd:d + (h + 1) * xd]
        s = _dot_nt(qh, kh) * (xd ** -0.5)
        e = jnp.exp(s - jnp.max(s, axis=-1, keepdims=True))
        p = e / jnp.sum(e, axis=-1, keepdims=True)
        outs.append(_dot(p.astype(BF16), vh))
    o = jnp.concatenate(outs, axis=-1).astype(BF16)
    x2 = x1 + _dot(o, wco_ref[...])
    x2_ref[...] = x2
    xp_ref[...] = _rms(x2, gpe_ref[...]).astype(BF16)


def _merge_call(x2d, osb, og, gates, kv, wsb, wg, wmix, g_cross, wcq, wco, g_peer, seq, n_mem, tm):
    t, d = x2d.shape
    per_b = seq // tm
    row = lambda i: (i, 0)
    fixed = lambda i: (0, 0)
    return pl.pallas_call(
        functools.partial(_merge_kernel, d=d),
        grid=(t // tm,),
        in_specs=[pl.BlockSpec((tm, d), row),
                  pl.BlockSpec((tm, SB_W), row),
                  pl.BlockSpec((tm, GDN_W), row),
                  pl.BlockSpec((tm, 2 * d), row),
                  pl.BlockSpec((n_mem, 2 * d), lambda i: (i // per_b, 0)),
                  pl.BlockSpec((SB_W, d), fixed),
                  pl.BlockSpec((GDN_W, d), fixed),
                  pl.BlockSpec((d, d), fixed),
                  pl.BlockSpec((1, d), fixed),
                  pl.BlockSpec((d, d), fixed),
                  pl.BlockSpec((d, d), fixed),
                  pl.BlockSpec((1, d), fixed)],
        out_specs=[pl.BlockSpec((tm, d), row), pl.BlockSpec((tm, d), row)],
        out_shape=[jax.ShapeDtypeStruct((t, d), F32), jax.ShapeDtypeStruct((t, d), BF16)],
        compiler_params=_params(("arbitrary",)),
        name="merge",
    )(x2d, osb, og, gates, kv, wsb, wg, wmix, g_cross, wcq, wco, g_peer)


N_HP = 2 * PEER_HEADS
N_CAND = PEER_TOPK * PEER_TOPK


def _peersel_kernel(xp_ref, wpq_ref, sk_ref, s2_ref, b2_ref, th_ref, a1_ref,
                    qps, sall, cur, tops, cand, best):
    qp = _dot(xp_ref[...], wpq_ref[...])
    for hp in range(N_HP):
        qps[hp] = qp[:, hp * PEER_KEYS:(hp + 1) * PEER_KEYS].astype(BF16)

    def top_rounds(src_ref, dst_ref, rounds):
        def rnd(r, carry):
            c = src_ref[...]
            m = jnp.max(c, axis=0, keepdims=True)
            dst_ref[pl.ds(r, 1), :] = m
            src_ref[...] = jnp.where(c == m, -jnp.inf, c)
            return carry
        lax.fori_loop(0, rounds, rnd, 0)

    def half(hp, carry):
        s = _dot_nt(sk_ref[hp], qps[hp])
        sall[hp] = s
        cur[...] = s
        top_rounds(cur, tops.at[hp], PEER_TOPK)
        return carry

    lax.fori_loop(0, N_HP, half, 0)

    def head(h, carry):
        xs = tops[2 * h]
        ys = tops[2 * h + 1]
        for a in range(PEER_TOPK):
            cand[a * PEER_TOPK:(a + 1) * PEER_TOPK, :] = xs[a:a + 1, :] + ys
        top_rounds(cand, best, PEER_TOPK + 1)
        bs = best[...]
        top = bs[0:PEER_TOPK, :]
        zsum = jnp.sum(jnp.exp(top - top[0:1, :]), axis=0, keepdims=True)
        theta = 0.5 * (bs[PEER_TOPK - 1:PEER_TOPK, :] + bs[PEER_TOPK:PEER_TOPK + 1, :])
        s1 = sall[2 * h]
        s2 = sall[2 * h + 1]
        s2_ref[h] = s2
        b2_ref[h] = jnp.exp(s2 - ys[0:1, :])
        th_ref[h] = theta - s1
        a1_ref[h] = jnp.exp(s1 - xs[0:1, :]) / zsum
        return carry

    lax.fori_loop(0, PEER_HEADS, head, 0)


def _peersel_call(xp, wpq, sk, tm):
    t, d = xp.shape
    nq = wpq.shape[1]
    kd = sk.shape[-1]
    out = jax.ShapeDtypeStruct((PEER_HEADS, PEER_KEYS, t), F32)
    ospec = pl.BlockSpec((PEER_HEADS, PEER_KEYS, tm), lambda i: (0, 0, i))
    return pl.pallas_call(
        _peersel_kernel,
        grid=(t // tm,),
        in_specs=[pl.BlockSpec((tm, d), lambda i: (i, 0)),
                  pl.BlockSpec((d, nq), lambda i: (0, 0)),
                  pl.BlockSpec((N_HP, PEER_KEYS, kd), lambda i: (0, 0, 0))],
        out_specs=[ospec, ospec, ospec, ospec],
        out_shape=[out, out, out, out],
        scratch_shapes=[pltpu.VMEM((N_HP, tm, kd), BF16),
                        pltpu.VMEM((N_HP, PEER_KEYS, tm), F32),
                        pltpu.VMEM((PEER_KEYS, tm), F32),
                        pltpu.VMEM((N_HP, PEER_TOPK, tm), F32),
                        pltpu.VMEM((N_CAND, tm), F32),
                        pltpu.VMEM((PEER_TOPK + 8, tm), F32)],
        compiler_params=_params(("arbitrary",)),
        name="peersel",
    )(xp, wpq, sk)


PEER_EBLK = 2 * PEER_KEYS
PEER_RC = 16


def _gelu(x):
    return 0.5 * x * (1.0 + lax.erf(x * 0.7071067811865476))


def _peer_kernel(xp_ref, u_ref, v_ref, s2_ref, b2_ref, th_ref, a1_ref, x2_ref, gf_ref, o_ref,
                 ht, at, acc):
    e = pl.program_id(1)

    @pl.when(e == 0)
    def _():
        acc[...] = jnp.zeros_like(acc)

    ht[...] = _dot_nt(u_ref[...], xp_ref[...])

    for half in range(2):
        i = 2 * e + half

        def rows(r, carry, half=half, i=i):
            j0 = pl.multiple_of(r * PEER_RC, PEER_RC)
            w = jnp.zeros((PEER_RC, ht.shape[1]), F32)
            for h in range(PEER_HEADS):
                thr = th_ref[h, pl.ds(i, 1), :]
                cf = a1_ref[h, pl.ds(i, 1), :]
                s2 = s2_ref[h, pl.ds(j0, PEER_RC), :]
                b2 = b2_ref[h, pl.ds(j0, PEER_RC), :]
                w = w + jnp.where(s2 >= thr, b2 * cf, 0.0)
            hrow = ht[pl.ds(half * PEER_KEYS + j0, PEER_RC), :]
            at[pl.ds(half * PEER_KEYS + j0, PEER_RC), :] = (_gelu(hrow) * w).astype(BF16)
            return carry

        lax.fori_loop(0, PEER_KEYS // PEER_RC, rows, 0)

    acc[...] += _dot_tn(v_ref[...], at[...])

    @pl.when(e == pl.num_programs(1) - 1)
    def _():
        x3 = x2_ref[...] + acc[...].T
        o_ref[...] = _rms(x3, gf_ref[...])


def _peer_call(xp, u16, v16, s2, b2, th, a1, x2, g_final, tb):
    t, d = xp.shape
    n_exp = u16.shape[0]
    tok = lambda i, e: (i, 0)
    sel = pl.BlockSpec((PEER_HEADS, PEER_KEYS, tb), lambda i, e: (0, 0, i))
    return pl.pallas_call(
        _peer_kernel,
        grid=(t // tb, n_exp // PEER_EBLK),
        in_specs=[pl.BlockSpec((tb, d), tok),
                  pl.BlockSpec((PEER_EBLK, d), lambda i, e: (e, 0)),
                  pl.BlockSpec((PEER_EBLK, d), lambda i, e: (e, 0)),
                  sel, sel, sel, sel,
                  pl.BlockSpec((tb, d), tok),
                  pl.BlockSpec((1, d), lambda i, e: (0, 0))],
        out_specs=pl.BlockSpec((tb, d), tok),
        out_shape=jax.ShapeDtypeStruct((t, d), F32),
        scratch_shapes=[pltpu.VMEM((PEER_EBLK, tb), F32),
                        pltpu.VMEM((PEER_EBLK, tb), BF16),
                        pltpu.VMEM((d, tb), F32)],
        compiler_params=_params(("arbitrary", "arbitrary")),
        name="peer",
    )(xp, u16, v16, s2, b2, th, a1, x2, g_final)


def _pick(n, pref):
    for c in pref:
        if n % c == 0:
            return c
    return n


def _regroup_w_in(w_in):
    d = w_in.shape[0]
    o_ba = 3 * SB_W + 4 * GDN_W
    ba = jnp.pad(w_in[:, o_ba:o_ba + 2 * GDN_HEADS], ((0, 0), (0, LANES - 2 * GDN_HEADS)))
    return jnp.concatenate([w_in[:, :o_ba], w_in[:, o_ba + 2 * GDN_HEADS:], ba], axis=1).astype(BF16)


def _layer(x2d, kv, batch, seq, n_mem, g_mix, w_in, gdn_conv, gdn_a_log, gdn_dt_bias, gdn_norm_g,
           w_sb_up, w_gdn_up, w_mix_out, g_cross, w_cq, w_co, g_peer, w_pq, peer_subkeys,
           peer_u, peer_v, g_out):
    t, d = x2d.shape
    tm = _pick(seq, (256, 128))
    sbqkv, gqkv, gz, gates, gba = _inproj_call(x2d, g_mix.reshape(1, d), _regroup_w_in(w_in), tm)
    o_sb = _sb_call(sbqkv, batch, seq)
    lane_pad = lambda v: jnp.pad(v.astype(F32), (GDN_HEADS, LANES - 2 * GDN_HEADS)).reshape(1, LANES)
    o_g = _gdn_call(gqkv, gz, gba, gdn_conv, lane_pad(gdn_a_log), lane_pad(gdn_dt_bias),
                    gdn_norm_g.reshape(1, GDN_HEAD_DIM), batch, seq, _pick(seq, (256, 128, 64)))
    x2, xp = _merge_call(x2d, o_sb, o_g, gates, kv, w_sb_up.astype(BF16), w_gdn_up.astype(BF16),
                         w_mix_out.astype(BF16), g_cross.reshape(1, d), w_cq.astype(BF16),
                         w_co.astype(BF16), g_peer.reshape(1, d), seq, n_mem, tm)
    sk = peer_subkeys.reshape(N_HP, PEER_KEYS, -1).astype(BF16)
    s2, b2, th, a1 = _peersel_call(xp, w_pq.astype(BF16), sk, tm)
    return _peer_call(xp, peer_u.astype(BF16), peer_v.astype(BF16), s2, b2, th, a1, x2,
                      g_out.reshape(1, d), _pick(t, (512, 256, 128)))


def kernel(x, mem, g_mix, w_in, gdn_conv, gdn_a_log, gdn_dt_bias, gdn_norm_g, w_sb_up, w_gdn_up,
           w_mix_out, g_cross, g_mem, w_cq, w_ckv, w_co, g_peer, w_pq, peer_subkeys, peer_u, peer_v,
           g_final):
    batch, seq, d = x.shape
    n_mem = mem.shape[1]
    depth = g_mix.shape[0]
    assert depth == 1, "the final RMSNorm is fused into the last layer's PEER kernel"
    x2d = x.reshape(batch * seq, d)
    mem2d = mem.reshape(batch * n_mem, d)
    l = 0
    kv = _kv_call(mem2d, g_mem[l].reshape(1, d), w_ckv[l].astype(BF16), n_mem)
    out = _layer(x2d, kv, batch, seq, n_mem, g_mix[l], w_in[l], gdn_conv[l], gdn_a_log[l],
                 gdn_dt_bias[l], gdn_norm_g[l], w_sb_up[l], w_gdn_up[l], w_mix_out[l], g_cross[l],
                 w_cq[l], w_co[l], g_peer[l], w_pq[l], peer_subkeys[l], peer_u[l], peer_v[l], g_final)
    return out.reshape(batch, seq, d)
```

```python
import functools

import jax
import jax.numpy as jnp
from jax import lax
from jax.experimental import pallas as pl
from jax.experimental.pallas import tpu as pltpu

F32 = jnp.float32
BF16 = jnp.bfloat16

EPS = 1e-6
SB_HEADS = 8
SB_HEAD_DIM = 64
GDN_HEADS = 4
GDN_HEAD_DIM = 128
CONV_WIDTH = 4
CHUNK = 64
X_HEADS = 4
PEER_HEADS = 8
PEER_KEYS = 128
PEER_TOPK = 16

LANES = 128
VMEM_LIMIT = 56 * 1024 * 1024
SB_SKIP_LOG = -104.0


def _dot(a, b):
    return jnp.dot(a, b, preferred_element_type=F32)


def _dot_nt(a, b):
    return lax.dot_general(a, b, (((1,), (1,)), ((), ())), preferred_element_type=F32)


def _dot_tn(a, b):
    return lax.dot_general(a, b, (((0,), (0,)), ((), ())), preferred_element_type=F32)


def _split3(a):
    hi = a.astype(BF16)
    r1 = a - hi.astype(F32)
    mid = r1.astype(BF16)
    lo = (r1 - mid.astype(F32)).astype(BF16)
    return hi, mid, lo


def _dot_exact_lhs(m_bf16, b):
    hi, mid, lo = _split3(b)
    return _dot(m_bf16, hi) + _dot(m_bf16, mid) + _dot(m_bf16, lo)


def _split2(a):
    hi = a.astype(BF16)
    return hi, (a - hi.astype(F32)).astype(BF16)


def _dot_hp2(a2, b2):
    return _dot(a2[0], b2[0]) + _dot(a2[0], b2[1]) + _dot(a2[1], b2[0])


def _rms(x, g):
    return x * lax.rsqrt(jnp.mean(x * x, axis=-1, keepdims=True) + EPS) * g


def _softplus(x):
    return jnp.maximum(x, 0.0) + jnp.log1p(jnp.exp(-jnp.abs(x)))


def _sigmoid(x):
    return 1.0 / (1.0 + jnp.exp(-x))


def _params(sem):
    return pltpu.CompilerParams(dimension_semantics=sem, vmem_limit_bytes=VMEM_LIMIT)


def _pick(n, pref):
    for c in pref:
        if n % c == 0:
            return c
    return n


def _kv_kernel(mem_ref, g_ref, w_ref, o_ref):
    mn = _rms(mem_ref[...], g_ref[...]).astype(BF16)
    o_ref[...] = _dot(mn, w_ref[...]).astype(BF16)


def _kv_call(mem2d, g_mem, w_ckv, n_mem):
    rows, d = mem2d.shape
    return pl.pallas_call(
        _kv_kernel,
        grid=(rows // n_mem,),
        in_specs=[pl.BlockSpec((n_mem, d), lambda i: (i, 0)),
                  pl.BlockSpec((1, d), lambda i: (0, 0)),
                  pl.BlockSpec((d, 2 * d), lambda i: (0, 0))],
        out_specs=pl.BlockSpec((n_mem, 2 * d), lambda i: (i, 0)),
        out_shape=jax.ShapeDtypeStruct((rows, 2 * d), BF16),
        compiler_params=_params(("arbitrary",)),
        name="kv",
    )(mem2d, g_mem, w_ckv)


SB_W = SB_HEADS * SB_HEAD_DIM
GDN_W = GDN_HEADS * GDN_HEAD_DIM
C_SB = (0, 3 * SB_W)
C_GQ = (C_SB[1], C_SB[1] + 3 * GDN_W)
C_GZ = (C_GQ[1], C_GQ[1] + GDN_W)


def _inproj_kernel(x_ref, g_ref, w_ref, sb_ref, gq_ref, gz_ref, gt_ref, ba_ref, *, d):
    xn = _rms(x_ref[...], g_ref[...]).astype(BF16)
    c_gt = (C_GZ[1], C_GZ[1] + 2 * d)
    c_ba = (c_gt[1], c_gt[1] + LANES)

    def mm(c):
        return _dot(xn, w_ref[:, c[0]:c[1]])

    sb_ref[...] = mm(C_SB).astype(BF16)
    gq_ref[...] = mm(C_GQ)
    gz_ref[...] = mm(C_GZ)
    gt_ref[...] = mm(c_gt).astype(BF16)
    ba_ref[...] = mm(c_ba)


def _inproj_call(x2d, g_mix, w_p, tm):
    t, d = x2d.shape
    n = w_p.shape[1]
    row = lambda i: (i, 0)
    fixed = lambda i: (0, 0)
    return pl.pallas_call(
        functools.partial(_inproj_kernel, d=d),
        grid=(t // tm,),
        in_specs=[pl.BlockSpec((tm, d), row),
                  pl.BlockSpec((1, d), fixed),
                  pl.BlockSpec((d, n), fixed)],
        out_specs=[pl.BlockSpec((tm, 3 * SB_W), row),
                   pl.BlockSpec((tm, 3 * GDN_W), row),
                   pl.BlockSpec((tm, GDN_W), row),
                   pl.BlockSpec((tm, 2 * d), row),
                   pl.BlockSpec((tm, LANES), row)],
        out_shape=[jax.ShapeDtypeStruct((t, 3 * SB_W), BF16),
                   jax.ShapeDtypeStruct((t, 3 * GDN_W), F32),
                   jax.ShapeDtypeStruct((t, GDN_W), F32),
                   jax.ShapeDtypeStruct((t, 2 * d), BF16),
                   jax.ShapeDtypeStruct((t, LANES), F32)],
        compiler_params=_params(("arbitrary",)),
        name="inproj",
    )(x2d, g_mix, w_p)


SB_GROUP = 2 * LANES
SB_HPG = SB_GROUP // SB_HEAD_DIM


def _sb_kernel(q_ref, k_ref, v_ref, o_ref, acc_ref, c_ref, *, seq, blk):
    scale = SB_HEAD_DIM ** -0.5
    lane = lax.broadcasted_iota(jnp.int32, (blk, SB_GROUP), 1)
    in_head = [jnp.logical_and(lane >= h * SB_HEAD_DIM, lane < (h + 1) * SB_HEAD_DIM)
               for h in range(SB_HPG)]
    r_i = lax.broadcasted_iota(jnp.int32, (blk, blk), 0)
    c_i = lax.broadcasted_iota(jnp.int32, (blk, blk), 1)
    causal = c_i < r_i
    jj = lax.broadcasted_iota(jnp.int32, (blk, 2 * blk), 0)
    ss = lax.broadcasted_iota(jnp.int32, (blk, 2 * blk), 1)
    m_ext = jnp.where(ss >= blk, 1.0, jnp.where(jj > ss, 1.0, 0.0)).astype(BF16)

    def tile(qm, kblk, vblk, hd, masked):
        z = _dot_nt(qm, kblk) * scale
        sp = _softplus(z)
        lk = -sp
        if masked:
            lk = jnp.where(causal, lk, 0.0)
        hi, mid = _split2(lk)
        ext = _dot(hi, m_ext) + _dot(mid, m_ext)
        after = ext[:, :blk]
        tot = ext[:, blk:]
        c = c_ref[hd]
        a = jnp.exp((z - sp) + after + c)
        if masked:
            a = jnp.where(causal, a, 0.0)
        acc_ref[hd] += _dot(a.astype(BF16), vblk)
        c_new = c + tot
        c_ref[hd] = c_new
        return c_new

    def all_heads(q, kblk, vblk, masked):
        zero = jnp.zeros_like(q)
        cm = None
        for h in range(SB_HPG):
            c_new = tile(jnp.where(in_head[h], q, zero), kblk, vblk, h, masked)
            cm = c_new if cm is None else jnp.maximum(cm, c_new)
        return jnp.max(cm)

    def qblock(qi, carry):
        q0 = pl.multiple_of(qi * blk, blk)
        q = q_ref[pl.ds(q0, blk), :]
        acc_ref[...] = jnp.zeros_like(acc_ref)
        c_ref[...] = jnp.zeros_like(c_ref)
        m0 = all_heads(q, k_ref[pl.ds(q0, blk), :], v_ref[pl.ds(q0, blk), :], True)

        def cond(st):
            kb, cm = st
            return jnp.logical_and(kb >= 0, cm > SB_SKIP_LOG)

        def body(st):
            kb, _ = st
            k0 = pl.multiple_of(kb * blk, blk)
            return kb - 1, all_heads(q, k_ref[pl.ds(k0, blk), :], v_ref[pl.ds(k0, blk), :], False)

        lax.while_loop(cond, body, (qi - 1, m0))
        o = acc_ref[0]
        for h in range(1, SB_HPG):
            o = jnp.where(in_head[h], acc_ref[h], o)
        o_ref[pl.ds(q0, blk), :] = o.astype(BF16)
        return carry

    lax.fori_loop(0, seq // blk, qblock, 0)


def _sb_call(sbqkv, batch, seq):
    t = sbqkv.shape[0]
    ng = SB_W // SB_GROUP
    blk = _pick(seq, (256, 128))
    return pl.pallas_call(
        functools.partial(_sb_kernel, seq=seq, blk=blk),
        grid=(batch, ng),
        in_specs=[pl.BlockSpec((seq, SB_GROUP), lambda b, g: (b, g)),
                  pl.BlockSpec((seq, SB_GROUP), lambda b, g: (b, ng + g)),
                  pl.BlockSpec((seq, SB_GROUP), lambda b, g: (b, 2 * ng + g))],
        out_specs=pl.BlockSpec((seq, SB_GROUP), lambda b, g: (b, g)),
        out_shape=jax.ShapeDtypeStruct((t, SB_W), BF16),
        scratch_shapes=[pltpu.VMEM((SB_HPG, blk, SB_GROUP), F32),
                        pltpu.VMEM((SB_HPG, blk, blk), F32)],
        compiler_params=_params(("arbitrary", "arbitrary")),
        name="sb",
    )(sbqkv, sbqkv, sbqkv)


GDN_PAD = 8


def _gdn_kernel(qkv_ref, z_ref, ba_ref, conv_ref, alog_ref, dtb_ref, ng_ref, o_ref,
                xbuf, u_s, w_s, qd_s, kd_s, qk_s, vn_s, os_, state, *, blk):
    hd = GDN_HEAD_DIM
    w3 = 3 * GDN_W
    nh = GDN_HEADS

    @pl.when(pl.program_id(1) == 0)
    def _():
        state[...] = jnp.zeros_like(state)
        xbuf[0:GDN_PAD, :] = jnp.zeros((GDN_PAD, w3), F32)

    xbuf[GDN_PAD:GDN_PAD + blk, :] = qkv_ref[...]
    cw = conv_ref[...]
    y = xbuf[GDN_PAD:GDN_PAD + blk, :] * cw[CONV_WIDTH - 1:CONV_WIDTH, :]
    for j in range(CONV_WIDTH - 1):
        d = CONV_WIDTH - 1 - j
        y = y + xbuf[GDN_PAD - d:GDN_PAD - d + blk, :] * cw[j:j + 1, :]
    xbuf[0:GDN_PAD, :] = xbuf[blk:blk + GDN_PAD, :]
    y = y * _sigmoid(y)

    ba = ba_ref[...]
    lane = lax.broadcasted_iota(jnp.int32, (blk, LANES), 1)
    g = -jnp.exp(alog_ref[...]) * _softplus(ba + dtb_ref[...])
    g = jnp.where(jnp.logical_and(lane >= nh, lane < 2 * nh), g, 0.0)
    rr = lax.broadcasted_iota(jnp.int32, (blk, blk), 0)
    cc = lax.broadcasted_iota(jnp.int32, (blk, blk), 1)
    same = (rr // CHUNK) == (cc // CHUNK)
    eye = jnp.where(rr == cc, 1.0, 0.0)
    bd_ones = jnp.where(same, 1.0, 0.0)
    bd_incl = jnp.where(cc <= rr, bd_ones, 0.0)
    bd_strict = bd_incl - eye
    gc = _dot_exact_lhs(bd_incl.astype(BF16), g)
    gl = _dot_exact_lhs(bd_ones.astype(BF16), g)
    eg = jnp.exp(gc)
    ek = jnp.exp(gl - gc)
    ec = jnp.exp(gl)
    beta = _sigmoid(ba)
    incl = bd_incl > 0.5

    for h in range(nh):
        qh = y[:, h * hd:(h + 1) * hd]
        kh = y[:, GDN_W + h * hd:GDN_W + (h + 1) * hd]
        vh = y[:, 2 * GDN_W + h * hd:2 * GDN_W + (h + 1) * hd]
        q = qh * (lax.rsqrt(jnp.sum(qh * qh, axis=-1, keepdims=True) + EPS) * (hd ** -0.5))
        k = kh * lax.rsqrt(jnp.sum(kh * kh, axis=-1, keepdims=True) + EPS)
        gcol = gc[:, nh + h:nh + h + 1]
        grow = jnp.sum(eye * gcol, axis=0, keepdims=True)
        dm = jnp.exp(jnp.where(incl, gcol - grow, -jnp.inf))
        bcol = beta[:, h:h + 1]
        kb = k * bcol
        vb = vh * bcol
        k16 = k.astype(BF16)
        a = _dot_nt(kb.astype(BF16), k16) * dm * bd_strict
        n2 = _split2(-a)
        x = eye - a
        p2 = _split2(_dot_hp2(n2, n2))
        for it in range(5):
            x = x + _dot_hp2(_split2(x), p2)
            if it < 4:
                p2 = _split2(_dot_hp2(p2, p2))
        egc = eg[:, nh + h:nh + h + 1]
        sol = _dot_hp2(_split2(x), _split2(jnp.concatenate([vb, kb * egc], axis=1)))
        u_s[h] = sol[:, :hd]
        w_s[h] = sol[:, hd:].astype(BF16)
        qk_s[h] = (_dot_nt(q.astype(BF16), k16) * dm).astype(BF16)
        qd_s[h] = (q * egc).astype(BF16)
        kd_s[h] = (k * ek[:, nh + h:nh + h + 1]).astype(BF16)

    vn_s[...] = jnp.zeros_like(vn_s)
    for c in range(blk // CHUNK):
        rows = slice(c * CHUNK, (c + 1) * CHUNK)
        for h in range(nh):
            s = state[h]
            s16 = s.astype(BF16)
            vnew = u_s[h, rows, :] - _dot(w_s[h, rows, :], s16)
            vn16 = vnew.astype(BF16)
            vn_s[h, rows, :] = vn16
            os_[rows, h * hd:(h + 1) * hd] = (_dot(qd_s[h, rows, :], s16)
                                              + _dot(qk_s[h, rows, :], vn_s[h]))
            cd = ec[c * CHUNK:c * CHUNK + 1, nh + h:nh + h + 1]
            state[h] = s * cd + _dot_tn(kd_s[h, rows, :], vn16)

    zg = z_ref[...]
    ng = ng_ref[...]
    for h in range(nh):
        cols = slice(h * hd, (h + 1) * hd)
        oh = _rms(os_[:, cols], ng)
        zh = zg[:, cols]
        o_ref[:, cols] = (oh * (zh * _sigmoid(zh))).astype(BF16)


def _gdn_call(gqkv, gz, gba, conv_w, alog_l, dtb_l, norm_g, batch, seq, blk):
    t = gqkv.shape[0]
    nb = seq // blk
    row = lambda b, i: (b * nb + i, 0)
    fixed = lambda b, i: (0, 0)
    w3 = 3 * GDN_W
    nh, hd = GDN_HEADS, GDN_HEAD_DIM
    return pl.pallas_call(
        functools.partial(_gdn_kernel, blk=blk),
        grid=(batch, nb),
        in_specs=[pl.BlockSpec((blk, w3), row),
                  pl.BlockSpec((blk, GDN_W), row),
                  pl.BlockSpec((blk, LANES), row),
                  pl.BlockSpec((CONV_WIDTH, w3), fixed),
                  pl.BlockSpec((1, LANES), fixed),
                  pl.BlockSpec((1, LANES), fixed),
                  pl.BlockSpec((1, hd), fixed)],
        out_specs=pl.BlockSpec((blk, GDN_W), row),
        out_shape=jax.ShapeDtypeStruct((t, GDN_W), BF16),
        scratch_shapes=[pltpu.VMEM((blk + GDN_PAD, w3), F32),
                        pltpu.VMEM((nh, blk, hd), F32),
                        pltpu.VMEM((nh, blk, hd), BF16),
                        pltpu.VMEM((nh, blk, hd), BF16),
                        pltpu.VMEM((nh, blk, hd), BF16),
                        pltpu.VMEM((nh, blk, blk), BF16),
                        pltpu.VMEM((nh, blk, hd), BF16),
                        pltpu.VMEM((blk, GDN_W), F32),
                        pltpu.VMEM((nh, hd, hd), F32)],
        compiler_params=_params(("arbitrary", "arbitrary")),
        name="gdn",
    )(gqkv, gz, gba, conv_w, alog_l, dtb_l, norm_g)


def _merge_kernel(x_ref, osb_ref, og_ref, gt_ref, kv_ref, wsb_ref, wg_ref, wmix_ref,
                  gcr_ref, wcq_ref, wco_ref, gpe_ref, x2_ref, xpt_ref, *, d):
    gates = gt_ref[...].astype(F32)
    merged = (_sigmoid(gates[:, :d]) * _dot(osb_ref[...], wsb_ref[...])
              + _sigmoid(gates[:, d:]) * _dot(og_ref[...], wg_ref[...]))
    x1 = x_ref[...] + _dot(merged.astype(BF16), wmix_ref[...])
    hn = _rms(x1, gcr_ref[...]).astype(BF16)
    q = _dot(hn, wcq_ref[...])
    xd = d // X_HEADS
    kv = kv_ref[...]
    outs = []
    for h in range(X_HEADS):
        qh = q[:, h * xd:(h + 1) * xd].astype(BF16)
        kh = kv[:, h * xd:(h + 1) * xd]
        vh = kv[:, d + h * xd:d + (h + 1) * xd]
        s = _dot_nt(qh, kh) * (xd ** -0.5)
        e = jnp.exp(s - jnp.max(s, axis=-1, keepdims=True))
        p = e / jnp.sum(e, axis=-1, keepdims=True)
        outs.append(_dot(p.astype(BF16), vh))
    o = jnp.concatenate(outs, axis=-1).astype(BF16)
    x2 = x1 + _dot(o, wco_ref[...])
    x2_ref[...] = x2
    xpt_ref[...] = _rms(x2, gpe_ref[...]).T.astype(BF16)


def _merge_call(x2d, osb, og, gates, kv, wsb, wg, wmix, g_cross, wcq, wco, g_peer, seq, n_mem, tm):
    t, d = x2d.shape
    per_b = seq // tm
    row = lambda i: (i, 0)
    fixed = lambda i: (0, 0)
    return pl.pallas_call(
        functools.partial(_merge_kernel, d=d),
        grid=(t // tm,),
        in_specs=[pl.BlockSpec((tm, d), row),
                  pl.BlockSpec((tm, SB_W), row),
                  pl.BlockSpec((tm, GDN_W), row),
                  pl.BlockSpec((tm, 2 * d), row),
                  pl.BlockSpec((n_mem, 2 * d), lambda i: (i // per_b, 0)),
                  pl.BlockSpec((SB_W, d), fixed),
                  pl.BlockSpec((GDN_W, d), fixed),
                  pl.BlockSpec((d, d), fixed),
                  pl.BlockSpec((1, d), fixed),
                  pl.BlockSpec((d, d), fixed),
                  pl.BlockSpec((d, d), fixed),
                  pl.BlockSpec((1, d), fixed)],
        out_specs=[pl.BlockSpec((tm, d), row), pl.BlockSpec((d, tm), lambda i: (0, i))],
        out_shape=[jax.ShapeDtypeStruct((t, d), F32), jax.ShapeDtypeStruct((d, t), BF16)],
        compiler_params=_params(("arbitrary",)),
        name="merge",
    )(x2d, osb, og, gates, kv, wsb, wg, wmix, g_cross, wcq, wco, g_peer)


N_HP = 2 * PEER_HEADS
N_TOP = PEER_TOPK + 1
PAIR_ROWS = [min(N_TOP, N_TOP // (a + 1)) for a in range(N_TOP)]
N_SINGLE = sum(1 for n in PAIR_ROWS if n == 1)
N_CAND = -(-sum(PAIR_ROWS) // 8) * 8
TOP_PAD = -(-N_TOP // 8) * 8


def _peersel_kernel(xpt_ref, wpqt_ref, sk_ref, s2_ref, b2_ref, th_ref, a1_ref,
                    qps, sall, cur, tops, cand, best):
    qpt = _dot(wpqt_ref[...], xpt_ref[...])
    for hp in range(N_HP):
        qps[hp] = qpt[hp * PEER_KEYS:(hp + 1) * PEER_KEYS, :].astype(BF16)

    def top_rounds(srcs, dsts, rounds):
        def rnd(r, carry):
            for src_ref, dst_ref in zip(srcs, dsts):
                c = src_ref[...]
                m = jnp.max(c, axis=0, keepdims=True)
                dst_ref[pl.ds(r, 1), :] = m
                src_ref[...] = jnp.where(c == m, -jnp.inf, c)
            return carry
        lax.fori_loop(0, rounds, rnd, 0)

    def head_scores(h, carry):
        for p in range(2):
            s = _dot(sk_ref[2 * h + p], qps[2 * h + p])
            sall[2 * h + p] = s
            cur[p] = s
        top_rounds([cur.at[0], cur.at[1]], [tops.at[2 * h], tops.at[2 * h + 1]], N_TOP)
        return carry

    lax.fori_loop(0, PEER_HEADS, head_scores, 0)

    n_multi = N_TOP - N_SINGLE
    cand[...] = jnp.full(cand.shape, -jnp.inf, F32)

    def head(h, carry):
        xs = tops[2 * h, 0:N_TOP, :]
        ys = tops[2 * h + 1, 0:N_TOP, :]
        row = 0
        for a in range(n_multi):
            cand[row:row + PAIR_ROWS[a], :] = xs[a:a + 1, :] + ys[0:PAIR_ROWS[a], :]
            row += PAIR_ROWS[a]
        cand[row:row + N_SINGLE, :] = xs[n_multi:N_TOP, :] + ys[0:1, :]
        top_rounds([cand], [best], N_TOP)
        bs = best[0:N_TOP, :]
        top = bs[0:PEER_TOPK, :]
        zsum = jnp.sum(jnp.exp(top - top[0:1, :]), axis=0, keepdims=True)
        theta = 0.5 * (bs[PEER_TOPK - 1:PEER_TOPK, :] + bs[PEER_TOPK:PEER_TOPK + 1, :])
        s1 = sall[2 * h]
        s2 = sall[2 * h + 1]
        s2_ref[h] = s2
        b2_ref[h] = jnp.exp(s2 - ys[0:1, :])
        th_ref[h] = theta - s1
        a1_ref[h] = jnp.exp(s1 - xs[0:1, :]) / zsum
        return carry

    lax.fori_loop(0, PEER_HEADS, head, 0)


def _peersel_call(xpt, wpqt, sk, tm):
    d, t = xpt.shape
    nq = wpqt.shape[0]
    kd = sk.shape[-1]
    out = jax.ShapeDtypeStruct((PEER_HEADS, PEER_KEYS, t), F32)
    ospec = pl.BlockSpec((PEER_HEADS, PEER_KEYS, tm), lambda i: (0, 0, i))
    return pl.pallas_call(
        _peersel_kernel,
        grid=(t // tm,),
        in_specs=[pl.BlockSpec((d, tm), lambda i: (0, i)),
                  pl.BlockSpec((nq, d), lambda i: (0, 0)),
                  pl.BlockSpec((N_HP, PEER_KEYS, kd), lambda i: (0, 0, 0))],
        out_specs=[ospec, ospec, ospec, ospec],
        out_shape=[out, out, out, out],
        scratch_shapes=[pltpu.VMEM((N_HP, kd, tm), BF16),
                        pltpu.VMEM((N_HP, PEER_KEYS, tm), F32),
                        pltpu.VMEM((2, PEER_KEYS, tm), F32),
                        pltpu.VMEM((N_HP, TOP_PAD, tm), F32),
                        pltpu.VMEM((N_CAND, tm), F32),
                        pltpu.VMEM((TOP_PAD, tm), F32)],
        compiler_params=_params(("arbitrary",)),
        name="peersel",
    )(xpt, wpqt, sk)


PEER_EBLK = 8 * PEER_KEYS
PEER_RC = 16


def _gelu(x):
    return 0.5 * x * (1.0 + lax.erf(x * 0.7071067811865476))


def _peer_kernel(xpt_ref, u_ref, vt_ref, s2_ref, b2_ref, th_ref, a1_ref, x2_ref, gf_ref, o_ref,
                 ht, at, acc):
    e = pl.program_id(1)

    @pl.when(e == 0)
    def _():
        acc[...] = jnp.zeros_like(acc)

    ht[...] = _dot(u_ref[...], xpt_ref[...])

    n_i = PEER_EBLK // PEER_KEYS
    for sub in range(n_i):
        i = n_i * e + sub
        thr = [th_ref[h, pl.ds(i, 1), :] for h in range(PEER_HEADS)]
        cf = [a1_ref[h, pl.ds(i, 1), :] for h in range(PEER_HEADS)]
        for r in range(PEER_KEYS // PEER_RC):
            j0 = r * PEER_RC
            w = None
            for h in range(PEER_HEADS):
                wh = jnp.where(s2_ref[h, j0:j0 + PEER_RC, :] >= thr[h],
                               b2_ref[h, j0:j0 + PEER_RC, :] * cf[h], 0.0)
                w = wh if w is None else w + wh
            row0 = sub * PEER_KEYS + j0
            at[row0:row0 + PEER_RC, :] = (_gelu(ht[row0:row0 + PEER_RC, :]) * w).astype(BF16)

    acc[...] += _dot(vt_ref[...], at[...])

    @pl.when(e == pl.num_programs(1) - 1)
    def _():
        x3 = x2_ref[...] + acc[...].T
        o_ref[...] = _rms(x3, gf_ref[...])


def _peer_call(xpt, u16, vt16, s2, b2, th, a1, x2, g_final, tb):
    d, t = xpt.shape
    n_exp = u16.shape[0]
    tok = lambda i, e: (i, 0)
    sel = pl.BlockSpec((PEER_HEADS, PEER_KEYS, tb), lambda i, e: (0, 0, i))
    return pl.pallas_call(
        _peer_kernel,
        grid=(t // tb, n_exp // PEER_EBLK),
        in_specs=[pl.BlockSpec((d, tb), lambda i, e: (0, i)),
                  pl.BlockSpec((PEER_EBLK, d), lambda i, e: (e, 0)),
                  pl.BlockSpec((d, PEER_EBLK), lambda i, e: (0, e)),
                  sel, sel, sel, sel,
                  pl.BlockSpec((tb, d), tok),
                  pl.BlockSpec((1, d), lambda i, e: (0, 0))],
        out_specs=pl.BlockSpec((tb, d), tok),
        out_shape=jax.ShapeDtypeStruct((t, d), F32),
        scratch_shapes=[pltpu.VMEM((PEER_EBLK, tb), F32),
                        pltpu.VMEM((PEER_EBLK, tb), BF16),
                        pltpu.VMEM((d, tb), F32)],
        compiler_params=_params(("arbitrary", "arbitrary")),
        name="peer",
    )(xpt, u16, vt16, s2, b2, th, a1, x2, g_final)


def _regroup_w_in(w_in):
    o_ba = 3 * SB_W + 4 * GDN_W
    ba = jnp.pad(w_in[:, o_ba:o_ba + 2 * GDN_HEADS], ((0, 0), (0, LANES - 2 * GDN_HEADS)))
    return jnp.concatenate([w_in[:, :o_ba], w_in[:, o_ba + 2 * GDN_HEADS:], ba], axis=1).astype(BF16)


def _layer(x2d, kv, batch, seq, n_mem, g_mix, w_in, gdn_conv, gdn_a_log, gdn_dt_bias, gdn_norm_g,
           w_sb_up, w_gdn_up, w_mix_out, g_cross, w_cq, w_co, g_peer, w_pq, peer_subkeys,
           peer_u, peer_v, g_out):
    t, d = x2d.shape
    tm = _pick(seq, (256, 128))
    sbqkv, gqkv, gz, gates, gba = _inproj_call(x2d, g_mix.reshape(1, d), _regroup_w_in(w_in), tm)
    o_sb = _sb_call(sbqkv, batch, seq)
    lane_pad = lambda v: jnp.pad(v.astype(F32), (GDN_HEADS, LANES - 2 * GDN_HEADS)).reshape(1, LANES)
    o_g = _gdn_call(gqkv, gz, gba, gdn_conv, lane_pad(gdn_a_log), lane_pad(gdn_dt_bias),
                    gdn_norm_g.reshape(1, GDN_HEAD_DIM), batch, seq, _pick(seq, (256, 128, 64)))
    x2, xpt = _merge_call(x2d, o_sb, o_g, gates, kv, w_sb_up.astype(BF16), w_gdn_up.astype(BF16),
                          w_mix_out.astype(BF16), g_cross.reshape(1, d), w_cq.astype(BF16),
                          w_co.astype(BF16), g_peer.reshape(1, d), seq, n_mem, tm)
    sk = peer_subkeys.reshape(N_HP, PEER_KEYS, -1).astype(BF16)
    s2, b2, th, a1 = _peersel_call(xpt, w_pq.T.astype(BF16), sk, tm)
    return _peer_call(xpt, peer_u.astype(BF16), peer_v.T.astype(BF16), s2, b2, th, a1, x2,
                      g_out.reshape(1, d), _pick(t, (512, 256, 128)))


def kernel(x, mem, g_mix, w_in, gdn_conv, gdn_a_log, gdn_dt_bias, gdn_norm_g, w_sb_up, w_gdn_up,
           w_mix_out, g_cross, g_mem, w_cq, w_ckv, w_co, g_peer, w_pq, peer_subkeys, peer_u, peer_v,
           g_final):
    batch, seq, d = x.shape
    n_mem = mem.shape[1]
    depth = g_mix.shape[0]
    assert depth == 1, "the final RMSNorm is fused into the last layer's PEER kernel"
    x2d = x.reshape(batch * seq, d)
    mem2d = mem.reshape(batch * n_mem, d)
    l = 0
    kv = _kv_call(mem2d, g_mem[l].reshape(1, d), w_ckv[l].astype(BF16), n_mem)
    out = _layer(x2d, kv, batch, seq, n_mem, g_mix[l], w_in[l], gdn_conv[l], gdn_a_log[l],
                 gdn_dt_bias[l], gdn_norm_g[l], w_sb_up[l], w_gdn_up[l], w_mix_out[l], g_cross[l],
                 w_cq[l], w_co[l], g_peer[l], w_pq[l], peer_subkeys[l], peer_u[l], peer_v[l], g_final)
    return out.reshape(batch, seq, d)
```

```python
import functools

import jax
import jax.numpy as jnp
from jax import lax
from jax.experimental import pallas as pl
from jax.experimental.pallas import tpu as pltpu

F32 = jnp.float32
BF16 = jnp.bfloat16

EPS = 1e-6
SB_HEADS = 8
SB_HEAD_DIM = 64
GDN_HEADS = 4
GDN_HEAD_DIM = 128
CONV_WIDTH = 4
CHUNK = 64
X_HEADS = 4
PEER_HEADS = 8
PEER_KEYS = 128
PEER_TOPK = 16

LANES = 128
SUBLANES = 8
VMEM_LIMIT = 56 * 1024 * 1024
SB_SKIP_LOG = -104.0


def _dot(a, b):
    return jnp.dot(a, b, preferred_element_type=F32)


def _dot_nt(a, b):
    return lax.dot_general(a, b, (((1,), (1,)), ((), ())), preferred_element_type=F32)


def _dot_tn(a, b):
    return lax.dot_general(a, b, (((0,), (0,)), ((), ())), preferred_element_type=F32)


def _split3(a):
    hi = a.astype(BF16)
    r1 = a - hi.astype(F32)
    mid = r1.astype(BF16)
    lo = (r1 - mid.astype(F32)).astype(BF16)
    return hi, mid, lo


def _dot_exact_lhs(m_bf16, b):
    hi, mid, lo = _split3(b)
    return _dot(m_bf16, hi) + _dot(m_bf16, mid) + _dot(m_bf16, lo)


def _split2(a):
    hi = a.astype(BF16)
    return hi, (a - hi.astype(F32)).astype(BF16)


def _dot_hp2(a2, b2):
    return _dot(a2[0], b2[0]) + _dot(a2[0], b2[1]) + _dot(a2[1], b2[0])


def _rms(x, g):
    return x * lax.rsqrt(jnp.mean(x * x, axis=-1, keepdims=True) + EPS) * g


def _softplus(x):
    return jnp.maximum(x, 0.0) + jnp.log1p(jnp.exp(-jnp.abs(x)))


def _sigmoid(x):
    return 1.0 / (1.0 + jnp.exp(-x))


def _params(sem):
    return pltpu.CompilerParams(dimension_semantics=sem, vmem_limit_bytes=VMEM_LIMIT)


def _pick(n, pref):
    for c in pref:
        if n % c == 0:
            return c
    return n


def _kv_kernel(mem_ref, g_ref, w_ref, o_ref):
    mn = _rms(mem_ref[...], g_ref[...]).astype(BF16)
    o_ref[...] = _dot(mn, w_ref[...]).astype(BF16)


def _kv_call(mem2d, g_mem, w_ckv, n_mem):
    rows, d = mem2d.shape
    return pl.pallas_call(
        _kv_kernel,
        grid=(rows // n_mem,),
        in_specs=[pl.BlockSpec((n_mem, d), lambda i: (i, 0)),
                  pl.BlockSpec((1, d), lambda i: (0, 0)),
                  pl.BlockSpec((d, 2 * d), lambda i: (0, 0))],
        out_specs=pl.BlockSpec((n_mem, 2 * d), lambda i: (i, 0)),
        out_shape=jax.ShapeDtypeStruct((rows, 2 * d), BF16),
        compiler_params=_params(("arbitrary",)),
        name="kv",
    )(mem2d, g_mem, w_ckv)


SB_W = SB_HEADS * SB_HEAD_DIM
GDN_W = GDN_HEADS * GDN_HEAD_DIM
C_SB = (0, 3 * SB_W)
C_GQ = (C_SB[1], C_SB[1] + 3 * GDN_W)
C_GZ = (C_GQ[1], C_GQ[1] + GDN_W)


def _inproj_kernel(x_ref, g_ref, w_ref, sb_ref, gq_ref, gz_ref, gt_ref, ba_ref, *, d):
    xn = _rms(x_ref[...], g_ref[...]).astype(BF16)
    c_gt = (C_GZ[1], C_GZ[1] + 2 * d)
    c_ba = (c_gt[1], c_gt[1] + LANES)

    def mm(c):
        return _dot(xn, w_ref[:, c[0]:c[1]])

    sb_ref[...] = mm(C_SB).astype(BF16)
    gq_ref[...] = mm(C_GQ)
    gz_ref[...] = mm(C_GZ)
    gt_ref[...] = mm(c_gt).astype(BF16)
    ba_ref[...] = mm(c_ba)


def _inproj_call(x2d, g_mix, w_p, tm):
    t, d = x2d.shape
    n = w_p.shape[1]
    row = lambda i: (i, 0)
    fixed = lambda i: (0, 0)
    return pl.pallas_call(
        functools.partial(_inproj_kernel, d=d),
        grid=(t // tm,),
        in_specs=[pl.BlockSpec((tm, d), row),
                  pl.BlockSpec((1, d), fixed),
                  pl.BlockSpec((d, n), fixed)],
        out_specs=[pl.BlockSpec((tm, 3 * SB_W), row),
                   pl.BlockSpec((tm, 3 * GDN_W), row),
                   pl.BlockSpec((tm, GDN_W), row),
                   pl.BlockSpec((tm, 2 * d), row),
                   pl.BlockSpec((tm, LANES), row)],
        out_shape=[jax.ShapeDtypeStruct((t, 3 * SB_W), BF16),
                   jax.ShapeDtypeStruct((t, 3 * GDN_W), F32),
                   jax.ShapeDtypeStruct((t, GDN_W), F32),
                   jax.ShapeDtypeStruct((t, 2 * d), BF16),
                   jax.ShapeDtypeStruct((t, LANES), F32)],
        compiler_params=_params(("arbitrary",)),
        name="inproj",
    )(x2d, g_mix, w_p)


SB_GROUP = 2 * LANES
SB_HPG = SB_GROUP // SB_HEAD_DIM


def _sb_kernel(q_ref, k_ref, v_ref, o_ref, acc_ref, c_ref, *, seq, blk):
    scale = SB_HEAD_DIM ** -0.5
    lane = lax.broadcasted_iota(jnp.int32, (blk, SB_GROUP), 1)
    in_head = [jnp.logical_and(lane >= h * SB_HEAD_DIM, lane < (h + 1) * SB_HEAD_DIM)
               for h in range(SB_HPG)]
    r_i = lax.broadcasted_iota(jnp.int32, (blk, blk), 0)
    c_i = lax.broadcasted_iota(jnp.int32, (blk, blk), 1)
    causal = c_i < r_i
    jj = lax.broadcasted_iota(jnp.int32, (blk, 2 * blk), 0)
    ss = lax.broadcasted_iota(jnp.int32, (blk, 2 * blk), 1)
    m_ext = jnp.where(ss >= blk, 1.0, jnp.where(jj > ss, 1.0, 0.0)).astype(BF16)

    def tile(qm, kblk, vblk, hd, masked):
        z = _dot_nt(qm, kblk) * scale
        sp = _softplus(z)
        lk = -sp
        if masked:
            lk = jnp.where(causal, lk, 0.0)
        hi, mid = _split2(lk)
        ext = _dot(hi, m_ext) + _dot(mid, m_ext)
        after = ext[:, :blk]
        tot = ext[:, blk:]
        c = c_ref[hd]
        a = jnp.exp((z - sp) + after + c)
        if masked:
            a = jnp.where(causal, a, 0.0)
        acc_ref[hd] += _dot(a.astype(BF16), vblk)
        c_new = c + tot
        c_ref[hd] = c_new
        return c_new

    def all_heads(q, kblk, vblk, masked):
        zero = jnp.zeros_like(q)
        cm = None
        for h in range(SB_HPG):
            c_new = tile(jnp.where(in_head[h], q, zero), kblk, vblk, h, masked)
            cm = c_new if cm is None else jnp.maximum(cm, c_new)
        return jnp.max(cm)

    def qblock(qi, carry):
        q0 = pl.multiple_of(qi * blk, blk)
        q = q_ref[pl.ds(q0, blk), :]
        acc_ref[...] = jnp.zeros_like(acc_ref)
        c_ref[...] = jnp.zeros_like(c_ref)
        m0 = all_heads(q, k_ref[pl.ds(q0, blk), :], v_ref[pl.ds(q0, blk), :], True)

        def cond(st):
            kb, cm = st
            return jnp.logical_and(kb >= 0, cm > SB_SKIP_LOG)

        def body(st):
            kb, _ = st
            k0 = pl.multiple_of(kb * blk, blk)
            return kb - 1, all_heads(q, k_ref[pl.ds(k0, blk), :], v_ref[pl.ds(k0, blk), :], False)

        lax.while_loop(cond, body, (qi - 1, m0))
        o = acc_ref[0]
        for h in range(1, SB_HPG):
            o = jnp.where(in_head[h], acc_ref[h], o)
        o_ref[pl.ds(q0, blk), :] = o.astype(BF16)
        return carry

    lax.fori_loop(0, seq // blk, qblock, 0)


def _sb_call(sbqkv, batch, seq):
    t = sbqkv.shape[0]
    ng = SB_W // SB_GROUP
    blk = _pick(seq, (256, 128))
    return pl.pallas_call(
        functools.partial(_sb_kernel, seq=seq, blk=blk),
        grid=(batch, ng),
        in_specs=[pl.BlockSpec((seq, SB_GROUP), lambda b, g: (b, g)),
                  pl.BlockSpec((seq, SB_GROUP), lambda b, g: (b, ng + g)),
                  pl.BlockSpec((seq, SB_GROUP), lambda b, g: (b, 2 * ng + g))],
        out_specs=pl.BlockSpec((seq, SB_GROUP), lambda b, g: (b, g)),
        out_shape=jax.ShapeDtypeStruct((t, SB_W), BF16),
        scratch_shapes=[pltpu.VMEM((SB_HPG, blk, SB_GROUP), F32),
                        pltpu.VMEM((SB_HPG, blk, blk), F32)],
        compiler_params=_params(("arbitrary", "arbitrary")),
        name="sb",
    )(sbqkv, sbqkv, sbqkv)


GDN_PAD = 8


def _gdn_kernel(qkv_ref, z_ref, ba_ref, conv_ref, alog_ref, dtb_ref, ng_ref, o_ref,
                xbuf, u_s, w_s, qd_s, kd_s, qk_s, vn_s, os_, state, *, blk):
    hd = GDN_HEAD_DIM
    w3 = 3 * GDN_W
    nh = GDN_HEADS

    @pl.when(pl.program_id(1) == 0)
    def _():
        state[...] = jnp.zeros_like(state)
        xbuf[0:GDN_PAD, :] = jnp.zeros((GDN_PAD, w3), F32)

    xbuf[GDN_PAD:GDN_PAD + blk, :] = qkv_ref[...]
    cw = conv_ref[...]
    y = xbuf[GDN_PAD:GDN_PAD + blk, :] * cw[CONV_WIDTH - 1:CONV_WIDTH, :]
    for j in range(CONV_WIDTH - 1):
        d = CONV_WIDTH - 1 - j
        y = y + xbuf[GDN_PAD - d:GDN_PAD - d + blk, :] * cw[j:j + 1, :]
    xbuf[0:GDN_PAD, :] = xbuf[blk:blk + GDN_PAD, :]
    y = y * _sigmoid(y)

    ba = ba_ref[...]
    lane = lax.broadcasted_iota(jnp.int32, (blk, LANES), 1)
    g = -jnp.exp(alog_ref[...]) * _softplus(ba + dtb_ref[...])
    g = jnp.where(jnp.logical_and(lane >= nh, lane < 2 * nh), g, 0.0)
    rr = lax.broadcasted_iota(jnp.int32, (blk, blk), 0)
    cc = lax.broadcasted_iota(jnp.int32, (blk, blk), 1)
    same = (rr // CHUNK) == (cc // CHUNK)
    eye = jnp.where(rr == cc, 1.0, 0.0)
    bd_ones = jnp.where(same, 1.0, 0.0)
    bd_incl = jnp.where(cc <= rr, bd_ones, 0.0)
    bd_strict = bd_incl - eye
    gc = _dot_exact_lhs(bd_incl.astype(BF16), g)
    gl = _dot_exact_lhs(bd_ones.astype(BF16), g)
    eg = jnp.exp(gc)
    ek = jnp.exp(gl - gc)
    ec = jnp.exp(gl)
    beta = _sigmoid(ba)
    incl = bd_incl > 0.5

    heads = range(nh)
    xs, p2s, rhs = [], [], []
    for h in heads:
        qh = y[:, h * hd:(h + 1) * hd]
        kh = y[:, GDN_W + h * hd:GDN_W + (h + 1) * hd]
        vh = y[:, 2 * GDN_W + h * hd:2 * GDN_W + (h + 1) * hd]
        q = qh * (lax.rsqrt(jnp.sum(qh * qh, axis=-1, keepdims=True) + EPS) * (hd ** -0.5))
        k = kh * lax.rsqrt(jnp.sum(kh * kh, axis=-1, keepdims=True) + EPS)
        gcol = gc[:, nh + h:nh + h + 1]
        grow = jnp.sum(eye * gcol, axis=0, keepdims=True)
        dm = jnp.exp(jnp.where(incl, gcol - grow, -jnp.inf))
        bcol = beta[:, h:h + 1]
        kb = k * bcol
        k16 = k.astype(BF16)
        a = _dot_nt(kb.astype(BF16), k16) * dm * bd_strict
        egc = eg[:, nh + h:nh + h + 1]
        qk_s[h] = (_dot_nt(q.astype(BF16), k16) * dm).astype(BF16)
        qd_s[h] = (q * egc).astype(BF16)
        kd_s[h] = (k * ek[:, nh + h:nh + h + 1]).astype(BF16)
        rhs.append(_split2(jnp.concatenate([vh * bcol, kb * egc], axis=1)))
        n2 = _split2(-a)
        xs.append(eye - a)
        p2s.append(_split2(_dot_hp2(n2, n2)))
    for it in range(5):
        xs = [xs[h] + _dot_hp2(_split2(xs[h]), p2s[h]) for h in heads]
        if it < 4:
            p2s = [_split2(_dot_hp2(p2s[h], p2s[h])) for h in heads]
    for h in heads:
        sol = _dot_hp2(_split2(xs[h]), rhs[h])
        u_s[h] = sol[:, :hd]
        w_s[h] = sol[:, hd:].astype(BF16)

    vn_s[...] = jnp.zeros_like(vn_s)
    for c in range(blk // CHUNK):
        rows = slice(c * CHUNK, (c + 1) * CHUNK)
        for h in range(nh):
            s = state[h]
            s16 = s.astype(BF16)
            vnew = u_s[h, rows, :] - _dot(w_s[h, rows, :], s16)
            vn16 = vnew.astype(BF16)
            vn_s[h, rows, :] = vn16
            os_[rows, h * hd:(h + 1) * hd] = (_dot(qd_s[h, rows, :], s16)
                                              + _dot(qk_s[h, rows, :], vn_s[h]))
            cd = ec[c * CHUNK:c * CHUNK + 1, nh + h:nh + h + 1]
            state[h] = s * cd + _dot_tn(kd_s[h, rows, :], vn16)

    zg = z_ref[...]
    ng = ng_ref[...]
    for h in range(nh):
        cols = slice(h * hd, (h + 1) * hd)
        oh = _rms(os_[:, cols], ng)
        zh = zg[:, cols]
        o_ref[:, cols] = (oh * (zh * _sigmoid(zh))).astype(BF16)


def _gdn_call(gqkv, gz, gba, conv_w, alog_l, dtb_l, norm_g, batch, seq, blk):
    t = gqkv.shape[0]
    nb = seq // blk
    row = lambda b, i: (b * nb + i, 0)
    fixed = lambda b, i: (0, 0)
    w3 = 3 * GDN_W
    nh, hd = GDN_HEADS, GDN_HEAD_DIM
    return pl.pallas_call(
        functools.partial(_gdn_kernel, blk=blk),
        grid=(batch, nb),
        in_specs=[pl.BlockSpec((blk, w3), row),
                  pl.BlockSpec((blk, GDN_W), row),
                  pl.BlockSpec((blk, LANES), row),
                  pl.BlockSpec((CONV_WIDTH, w3), fixed),
                  pl.BlockSpec((1, LANES), fixed),
                  pl.BlockSpec((1, LANES), fixed),
                  pl.BlockSpec((1, hd), fixed)],
        out_specs=pl.BlockSpec((blk, GDN_W), row),
        out_shape=jax.ShapeDtypeStruct((t, GDN_W), BF16),
        scratch_shapes=[pltpu.VMEM((blk + GDN_PAD, w3), F32),
                        pltpu.VMEM((nh, blk, hd), F32),
                        pltpu.VMEM((nh, blk, hd), BF16),
                        pltpu.VMEM((nh, blk, hd), BF16),
                        pltpu.VMEM((nh, blk, hd), BF16),
                        pltpu.VMEM((nh, blk, blk), BF16),
                        pltpu.VMEM((nh, blk, hd), BF16),
                        pltpu.VMEM((blk, GDN_W), F32),
                        pltpu.VMEM((nh, hd, hd), F32)],
        compiler_params=_params(("arbitrary", "arbitrary")),
        name="gdn",
    )(gqkv, gz, gba, conv_w, alog_l, dtb_l, norm_g)


def _merge_kernel(x_ref, osb_ref, og_ref, gt_ref, kv_ref, wsb_ref, wg_ref, wmix_ref,
                  gcr_ref, wcq_ref, wco_ref, gpe_ref, x2_ref, xpt_ref, *, d):
    gates = gt_ref[...].astype(F32)
    merged = (_sigmoid(gates[:, :d]) * _dot(osb_ref[...], wsb_ref[...])
              + _sigmoid(gates[:, d:]) * _dot(og_ref[...], wg_ref[...]))
    x1 = x_ref[...] + _dot(merged.astype(BF16), wmix_ref[...])
    hn = _rms(x1, gcr_ref[...]).astype(BF16)
    q = _dot(hn, wcq_ref[...])
    xd = d // X_HEADS
    kv = kv_ref[...]
    outs = []
    for h in range(X_HEADS):
        qh = q[:, h * xd:(h + 1) * xd].astype(BF16)
        kh = kv[:, h * xd:(h + 1) * xd]
        vh = kv[:, d + h * xd:d + (h + 1) * xd]
        s = _dot_nt(qh, kh) * (xd ** -0.5)
        e = jnp.exp(s - jnp.max(s, axis=-1, keepdims=True))
        p = e / jnp.sum(e, axis=-1, keepdims=True)
        outs.append(_dot(p.astype(BF16), vh))
    o = jnp.concatenate(outs, axis=-1).astype(BF16)
    x2 = x1 + _dot(o, wco_ref[...])
    x2_ref[...] = x2
    xpt_ref[...] = _rms(x2, gpe_ref[...]).T.astype(BF16)


def _merge_call(x2d, osb, og, gates, kv, wsb, wg, wmix, g_cross, wcq, wco, g_peer, seq, n_mem, tm):
    t, d = x2d.shape
    per_b = seq // tm
    row = lambda i: (i, 0)
    fixed = lambda i: (0, 0)
    return pl.pallas_call(
        functools.partial(_merge_kernel, d=d),
        grid=(t // tm,),
        in_specs=[pl.BlockSpec((tm, d), row),
                  pl.BlockSpec((tm, SB_W), row),
                  pl.BlockSpec((tm, GDN_W), row),
                  pl.BlockSpec((tm, 2 * d), row),
                  pl.BlockSpec((n_mem, 2 * d), lambda i: (i // per_b, 0)),
                  pl.BlockSpec((SB_W, d), fixed),
                  pl.BlockSpec((GDN_W, d), fixed),
                  pl.BlockSpec((d, d), fixed),
                  pl.BlockSpec((1, d), fixed),
                  pl.BlockSpec((d, d), fixed),
                  pl.BlockSpec((d, d), fixed),
                  pl.BlockSpec((1, d), fixed)],
        out_specs=[pl.BlockSpec((tm, d), row), pl.BlockSpec((d, tm), lambda i: (0, i))],
        out_shape=[jax.ShapeDtypeStruct((t, d), F32), jax.ShapeDtypeStruct((d, t), BF16)],
        compiler_params=_params(("arbitrary",)),
        name="merge",
    )(x2d, osb, og, gates, kv, wsb, wg, wmix, g_cross, wcq, wco, g_peer)


LOG2E = 1.4426950408889634
N_HP = 2 * PEER_HEADS
N_TOP = PEER_TOPK + 1
PAIR_ROWS = [min(N_TOP, N_TOP // (a + 1)) for a in range(N_TOP)]
N_SINGLE = sum(1 for n in PAIR_ROWS if n == 1)
N_CAND = -(-sum(PAIR_ROWS) // 8) * 8
TOP_PAD = -(-N_TOP // 8) * 8


def _sort_network(n):
    pairs = []
    p = 1
    while p < n:
        k = p
        while k >= 1:
            for j in range(k % p, n - k, 2 * k):
                for i in range(min(k, n - j - k)):
                    if (i + j) // (2 * p) == (i + j + k) // (2 * p):
                        pairs.append((i + j, i + j + k))
            k //= 2
        p *= 2
    return pairs


def _peersel_kernel(xpt_ref, wpqt_ref, sk_ref, s2_ref, th_ref, a1_ref,
                    qps, sall, tops, cand, best):
    qpt = _dot(wpqt_ref[...], xpt_ref[...])
    for hp in range(N_HP):
        qps[hp] = qpt[hp * PEER_KEYS:(hp + 1) * PEER_KEYS, :].astype(BF16)

    def top_rounds(srcs, dsts, rounds):
        def rnd(r, carry):
            for src_ref, dst_ref in zip(srcs, dsts):
                c = src_ref[...]
                m = jnp.max(c, axis=0, keepdims=True)
                dst_ref[pl.ds(r, 1), :] = m
                src_ref[...] = jnp.where(c == m, -jnp.inf, c)
            return carry
        lax.fori_loop(0, rounds, rnd, 0)

    def top_values(s, dst_ref):
        sub = s.shape[0] // SUBLANES
        v = [s[k * SUBLANES:(k + 1) * SUBLANES, :] for k in range(sub)]
        for a, b in _sort_network(sub):
            v[a], v[b] = jnp.maximum(v[a], v[b]), jnp.minimum(v[a], v[b])
        for r in range(N_TOP):
            m = jnp.max(v[0], axis=0, keepdims=True)
            dst_ref[r:r + 1, :] = m
            hit = v[0] == m
            for k in range(min(sub, N_TOP - r - 1)):
                nxt = v[k + 1] if k + 1 < sub else jnp.full_like(v[k], -jnp.inf)
                v[k] = jnp.where(hit, nxt, v[k])

    def head_scores(h, carry):
        for p in range(2):
            s = _dot(sk_ref[2 * h + p], qps[2 * h + p])
            sall[2 * h + p] = s
            top_values(s, tops.at[2 * h + p])
        return carry

    lax.fori_loop(0, PEER_HEADS, head_scores, 0)

    n_multi = N_TOP - N_SINGLE
    cand[...] = jnp.full(cand.shape, -jnp.inf, F32)

    def head(h, carry):
        xs = tops[2 * h, 0:N_TOP, :]
        ys = tops[2 * h + 1, 0:N_TOP, :]
        row = 0
        for a in range(n_multi):
            cand[row:row + PAIR_ROWS[a], :] = xs[a:a + 1, :] + ys[0:PAIR_ROWS[a], :]
            row += PAIR_ROWS[a]
        cand[row:row + N_SINGLE, :] = xs[n_multi:N_TOP, :] + ys[0:1, :]
        top_rounds([cand], [best], N_TOP)
        bs = best[0:N_TOP, :]
        top = bs[0:PEER_TOPK, :]
        zsum = jnp.sum(jnp.exp(top - top[0:1, :]), axis=0, keepdims=True)
        theta = 0.5 * (bs[PEER_TOPK - 1:PEER_TOPK, :] + bs[PEER_TOPK:PEER_TOPK + 1, :])
        s1 = sall[2 * h]
        s2 = sall[2 * h + 1]
        s2_ref[h] = (s2 - ys[0:1, :]) * LOG2E
        th_ref[h] = ((theta - ys[0:1, :]) - s1) * LOG2E
        a1_ref[h] = jnp.exp(s1 - xs[0:1, :]) / zsum
        return carry

    lax.fori_loop(0, PEER_HEADS, head, 0)


def _peersel_call(xpt, wpqt, sk, tm):
    d, t = xpt.shape
    nq = wpqt.shape[0]
    kd = sk.shape[-1]
    out = jax.ShapeDtypeStruct((PEER_HEADS, PEER_KEYS, t), F32)
    ospec = pl.BlockSpec((PEER_HEADS, PEER_KEYS, tm), lambda i: (0, 0, i))
    return pl.pallas_call(
        _peersel_kernel,
        grid=(t // tm,),
        in_specs=[pl.BlockSpec((d, tm), lambda i: (0, i)),
                  pl.BlockSpec((nq, d), lambda i: (0, 0)),
                  pl.BlockSpec((N_HP, PEER_KEYS, kd), lambda i: (0, 0, 0))],
        out_specs=[ospec, ospec, ospec],
        out_shape=[out, out, out],
        scratch_shapes=[pltpu.VMEM((N_HP, kd, tm), BF16),
                        pltpu.VMEM((N_HP, PEER_KEYS, tm), F32),
                        pltpu.VMEM((N_HP, TOP_PAD, tm), F32),
                        pltpu.VMEM((N_CAND, tm), F32),
                        pltpu.VMEM((TOP_PAD, tm), F32)],
        compiler_params=_params(("arbitrary",)),
        name="peersel",
    )(xpt, wpqt, sk)


PEER_EBLK = 8 * PEER_KEYS
PEER_GRP = 2 * PEER_KEYS
PEER_RC = 16


def _gelu(x):
    return 0.5 * x * (1.0 + lax.erf(x * 0.7071067811865476))


def _peer_kernel(xpt_ref, u_ref, vt_ref, s2_ref, th_ref, a1_ref, x2_ref, gf_ref, o_ref,
                 ht, at, acc):
    e = pl.program_id(1)

    @pl.when(e == 0)
    def _():
        acc[...] = jnp.zeros_like(acc)

    n_i = PEER_EBLK // PEER_KEYS
    n_grp = PEER_EBLK // PEER_GRP

    def scores(g):
        rows = slice(g * PEER_GRP, (g + 1) * PEER_GRP)
        ht[rows, :] = _dot(u_ref[rows, :], xpt_ref[...])

    def gates(sub):
        i = n_i * e + sub
        for lt in range(ht.shape[1] // LANES):
            tl = slice(lt * LANES, (lt + 1) * LANES)
            thr = [th_ref[h, pl.ds(i, 1), :][:, tl] for h in range(PEER_HEADS)]
            cf = [a1_ref[h, pl.ds(i, 1), :][:, tl] for h in range(PEER_HEADS)]
            for r in range(PEER_KEYS // PEER_RC):
                j0 = r * PEER_RC
                w = None
                for h in range(PEER_HEADS):
                    s2 = s2_ref[h, j0:j0 + PEER_RC, tl]
                    wh = jnp.where(s2 >= thr[h], jnp.exp2(s2) * cf[h], 0.0)
                    w = wh if w is None else w + wh
                row0 = sub * PEER_KEYS + j0
                at[row0:row0 + PEER_RC, tl] = (_gelu(ht[row0:row0 + PEER_RC, tl]) * w).astype(BF16)

    scores(0)
    for g in range(n_grp):
        if g + 1 < n_grp:
            scores(g + 1)
        for sub in range(g * PEER_GRP // PEER_KEYS, (g + 1) * PEER_GRP // PEER_KEYS):
            gates(sub)
        rows = slice(g * PEER_GRP, (g + 1) * PEER_GRP)
        acc[...] += _dot(vt_ref[:, rows], at[rows, :])

    @pl.when(e == pl.num_programs(1) - 1)
    def _():
        x3 = x2_ref[...] + acc[...].T
        o_ref[...] = _rms(x3, gf_ref[...])


def _peer_call(xpt, u16, vt16, s2, th, a1, x2, g_final, tb):
    d, t = xpt.shape
    n_exp = u16.shape[0]
    tok = lambda i, e: (i, 0)
    sel = pl.BlockSpec((PEER_HEADS, PEER_KEYS, tb), lambda i, e: (0, 0, i))
    return pl.pallas_call(
        _peer_kernel,
        grid=(t // tb, n_exp // PEER_EBLK),
        in_specs=[pl.BlockSpec((d, tb), lambda i, e: (0, i)),
                  pl.BlockSpec((PEER_EBLK, d), lambda i, e: (e, 0)),
                  pl.BlockSpec((d, PEER_EBLK), lambda i, e: (0, e)),
                  sel, sel, sel,
                  pl.BlockSpec((tb, d), tok),
                  pl.BlockSpec((1, d), lambda i, e: (0, 0))],
        out_specs=pl.BlockSpec((tb, d), tok),
        out_shape=jax.ShapeDtypeStruct((t, d), F32),
        scratch_shapes=[pltpu.VMEM((PEER_EBLK, tb), F32),
                        pltpu.VMEM((PEER_EBLK, tb), BF16),
                        pltpu.VMEM((d, tb), F32)],
        compiler_params=_params(("arbitrary", "arbitrary")),
        name="peer",
    )(xpt, u16, vt16, s2, th, a1, x2, g_final)


def _regroup_w_in(w_in):
    o_ba = 3 * SB_W + 4 * GDN_W
    ba = jnp.pad(w_in[:, o_ba:o_ba + 2 * GDN_HEADS], ((0, 0), (0, LANES - 2 * GDN_HEADS)))
    return jnp.concatenate([w_in[:, :o_ba], w_in[:, o_ba + 2 * GDN_HEADS:], ba], axis=1).astype(BF16)


def _layer(x2d, kv, batch, seq, n_mem, g_mix, w_in, gdn_conv, gdn_a_log, gdn_dt_bias, gdn_norm_g,
           w_sb_up, w_gdn_up, w_mix_out, g_cross, w_cq, w_co, g_peer, w_pq, peer_subkeys,
           peer_u, peer_v, g_out):
    t, d = x2d.shape
    tm = _pick(seq, (256, 128))
    sbqkv, gqkv, gz, gates, gba = _inproj_call(x2d, g_mix.reshape(1, d), _regroup_w_in(w_in), tm)
    o_sb = _sb_call(sbqkv, batch, seq)
    lane_pad = lambda v: jnp.pad(v.astype(F32), (GDN_HEADS, LANES - 2 * GDN_HEADS)).reshape(1, LANES)
    o_g = _gdn_call(gqkv, gz, gba, gdn_conv, lane_pad(gdn_a_log), lane_pad(gdn_dt_bias),
                    gdn_norm_g.reshape(1, GDN_HEAD_DIM), batch, seq, _pick(seq, (256, 128, 64)))
    x2, xpt = _merge_call(x2d, o_sb, o_g, gates, kv, w_sb_up.astype(BF16), w_gdn_up.astype(BF16),
                          w_mix_out.astype(BF16), g_cross.reshape(1, d), w_cq.astype(BF16),
                          w_co.astype(BF16), g_peer.reshape(1, d), seq, n_mem, tm)
    sk = peer_subkeys.reshape(N_HP, PEER_KEYS, -1).astype(BF16)
    s2, th, a1 = _peersel_call(xpt, w_pq.T.astype(BF16), sk, tm)
    return _peer_call(xpt, peer_u.astype(BF16), peer_v.T.astype(BF16), s2, th, a1, x2,
                      g_out.reshape(1, d), _pick(t, (512, 256, 128)))


def kernel(x, mem, g_mix, w_in, gdn_conv, gdn_a_log, gdn_dt_bias, gdn_norm_g, w_sb_up, w_gdn_up,
           w_mix_out, g_cross, g_mem, w_cq, w_ckv, w_co, g_peer, w_pq, peer_subkeys, peer_u, peer_v,
           g_final):
    batch, seq, d = x.shape
    n_mem = mem.shape[1]
    depth = g_mix.shape[0]
    assert depth == 1, "the final RMSNorm is fused into the last layer's PEER kernel"
    x2d = x.reshape(batch * seq, d)
    mem2d = mem.reshape(batch * n_mem, d)
    l = 0
    kv = _kv_call(mem2d, g_mem[l].reshape(1, d), w_ckv[l].astype(BF16), n_mem)
    out = _layer(x2d, kv, batch, seq, n_mem, g_mix[l], w_in[l], gdn_conv[l], gdn_a_log[l],
                 gdn_dt_bias[l], gdn_norm_g[l], w_sb_up[l], w_gdn_up[l], w_mix_out[l], g_cross[l],
                 w_cq[l], w_co[l], g_peer[l], w_pq[l], peer_subkeys[l], peer_u[l], peer_v[l], g_final)
    return out.reshape(batch, seq, d)
```

```python
import functools

import jax
import jax.numpy as jnp
from jax import lax
from jax.experimental import pallas as pl
from jax.experimental.pallas import tpu as pltpu

F32 = jnp.float32
BF16 = jnp.bfloat16

EPS = 1e-6
SB_HEADS = 8
SB_HEAD_DIM = 64
GDN_HEADS = 4
GDN_HEAD_DIM = 128
CONV_WIDTH = 4
CHUNK = 64
X_HEADS = 4
PEER_HEADS = 8
PEER_KEYS = 128
PEER_TOPK = 16

LANES = 128
SUBLANES = 8
VMEM_LIMIT = 56 * 1024 * 1024
SB_SKIP_LOG = -104.0


def _dot(a, b):
    return jnp.dot(a, b, preferred_element_type=F32)


def _dot_nt(a, b):
    return lax.dot_general(a, b, (((1,), (1,)), ((), ())), preferred_element_type=F32)


def _dot_tn(a, b):
    return lax.dot_general(a, b, (((0,), (0,)), ((), ())), preferred_element_type=F32)


def _split3(a):
    hi = a.astype(BF16)
    r1 = a - hi.astype(F32)
    mid = r1.astype(BF16)
    lo = (r1 - mid.astype(F32)).astype(BF16)
    return hi, mid, lo


def _dot_exact_lhs(m_bf16, b):
    hi, mid, lo = _split3(b)
    return _dot(m_bf16, hi) + _dot(m_bf16, mid) + _dot(m_bf16, lo)


def _split2(a):
    hi = a.astype(BF16)
    return hi, (a - hi.astype(F32)).astype(BF16)


def _dot_hp2(a2, b2):
    return _dot(a2[0], b2[0]) + _dot(a2[0], b2[1]) + _dot(a2[1], b2[0])


def _rms(x, g):
    return x * lax.rsqrt(jnp.mean(x * x, axis=-1, keepdims=True) + EPS) * g


def _softplus(x):
    return jnp.maximum(x, 0.0) + jnp.log1p(jnp.exp(-jnp.abs(x)))


def _sigmoid(x):
    return 1.0 / (1.0 + jnp.exp(-x))


def _params(sem):
    return pltpu.CompilerParams(dimension_semantics=sem, vmem_limit_bytes=VMEM_LIMIT)


def _pick(n, pref):
    for c in pref:
        if n % c == 0:
            return c
    return n


def _kv_kernel(mem_ref, g_ref, w_ref, o_ref):
    mn = _rms(mem_ref[...], g_ref[...]).astype(BF16)
    o_ref[...] = _dot(mn, w_ref[...]).astype(BF16)


def _kv_call(mem2d, g_mem, w_ckv, n_mem):
    rows, d = mem2d.shape
    return pl.pallas_call(
        _kv_kernel,
        grid=(rows // n_mem,),
        in_specs=[pl.BlockSpec((n_mem, d), lambda i: (i, 0)),
                  pl.BlockSpec((1, d), lambda i: (0, 0)),
                  pl.BlockSpec((d, 2 * d), lambda i: (0, 0))],
        out_specs=pl.BlockSpec((n_mem, 2 * d), lambda i: (i, 0)),
        out_shape=jax.ShapeDtypeStruct((rows, 2 * d), BF16),
        compiler_params=_params(("arbitrary",)),
        name="kv",
    )(mem2d, g_mem, w_ckv)


SB_W = SB_HEADS * SB_HEAD_DIM
GDN_W = GDN_HEADS * GDN_HEAD_DIM
C_SB = (0, 3 * SB_W)
C_GQ = (C_SB[1], C_SB[1] + 3 * GDN_W)
C_GZ = (C_GQ[1], C_GQ[1] + GDN_W)


def _inproj_kernel(x_ref, g_ref, w_ref, sb_ref, gq_ref, gz_ref, gt_ref, ba_ref, *, d):
    xn = _rms(x_ref[...], g_ref[...]).astype(BF16)
    c_gt = (C_GZ[1], C_GZ[1] + 2 * d)
    c_ba = (c_gt[1], c_gt[1] + LANES)

    def mm(c):
        return _dot(xn, w_ref[:, c[0]:c[1]])

    sb_ref[...] = mm(C_SB).astype(BF16)
    gq_ref[...] = mm(C_GQ)
    gz_ref[...] = mm(C_GZ)
    gt_ref[...] = mm(c_gt).astype(BF16)
    ba_ref[...] = mm(c_ba)


def _inproj_call(x2d, g_mix, w_p, tm):
    t, d = x2d.shape
    n = w_p.shape[1]
    row = lambda i: (i, 0)
    fixed = lambda i: (0, 0)
    return pl.pallas_call(
        functools.partial(_inproj_kernel, d=d),
        grid=(t // tm,),
        in_specs=[pl.BlockSpec((tm, d), row),
                  pl.BlockSpec((1, d), fixed),
                  pl.BlockSpec((d, n), fixed)],
        out_specs=[pl.BlockSpec((tm, 3 * SB_W), row),
                   pl.BlockSpec((tm, 3 * GDN_W), row),
                   pl.BlockSpec((tm, GDN_W), row),
                   pl.BlockSpec((tm, 2 * d), row),
                   pl.BlockSpec((tm, LANES), row)],
        out_shape=[jax.ShapeDtypeStruct((t, 3 * SB_W), BF16),
                   jax.ShapeDtypeStruct((t, 3 * GDN_W), F32),
                   jax.ShapeDtypeStruct((t, GDN_W), F32),
                   jax.ShapeDtypeStruct((t, 2 * d), BF16),
                   jax.ShapeDtypeStruct((t, LANES), F32)],
        compiler_params=_params(("arbitrary",)),
        name="inproj",
    )(x2d, g_mix, w_p)


SB_GROUP = 2 * LANES
SB_HPG = SB_GROUP // SB_HEAD_DIM


def _sb_kernel(q_ref, k_ref, v_ref, o_ref, acc_ref, c_ref, *, seq, blk):
    scale = SB_HEAD_DIM ** -0.5
    lane = lax.broadcasted_iota(jnp.int32, (blk, SB_GROUP), 1)
    in_head = [jnp.logical_and(lane >= h * SB_HEAD_DIM, lane < (h + 1) * SB_HEAD_DIM)
               for h in range(SB_HPG)]
    r_i = lax.broadcasted_iota(jnp.int32, (blk, blk), 0)
    c_i = lax.broadcasted_iota(jnp.int32, (blk, blk), 1)
    causal = c_i < r_i
    jj = lax.broadcasted_iota(jnp.int32, (blk, 2 * blk), 0)
    ss = lax.broadcasted_iota(jnp.int32, (blk, 2 * blk), 1)
    m_ext = jnp.where(ss >= blk, 1.0, jnp.where(jj > ss, 1.0, 0.0)).astype(BF16)

    def tile(qm, kblk, vblk, hd, masked):
        z = _dot_nt(qm, kblk) * scale
        sp = _softplus(z)
        lk = -sp
        if masked:
            lk = jnp.where(causal, lk, 0.0)
        hi, mid = _split2(lk)
        ext = _dot(hi, m_ext) + _dot(mid, m_ext)
        after = ext[:, :blk]
        tot = ext[:, blk:]
        c = c_ref[hd]
        a = jnp.exp((z - sp) + after + c)
        if masked:
            a = jnp.where(causal, a, 0.0)
        acc_ref[hd] += _dot(a.astype(BF16), vblk)
        c_new = c + tot
        c_ref[hd] = c_new
        return c_new

    def all_heads(q, kblk, vblk, masked):
        zero = jnp.zeros_like(q)
        cm = None
        for h in range(SB_HPG):
            c_new = tile(jnp.where(in_head[h], q, zero), kblk, vblk, h, masked)
            cm = c_new if cm is None else jnp.maximum(cm, c_new)
        return jnp.max(cm)

    def qblock(qi, carry):
        q0 = pl.multiple_of(qi * blk, blk)
        q = q_ref[pl.ds(q0, blk), :]
        acc_ref[...] = jnp.zeros_like(acc_ref)
        c_ref[...] = jnp.zeros_like(c_ref)
        m0 = all_heads(q, k_ref[pl.ds(q0, blk), :], v_ref[pl.ds(q0, blk), :], True)

        def cond(st):
            kb, cm = st
            return jnp.logical_and(kb >= 0, cm > SB_SKIP_LOG)

        def body(st):
            kb, _ = st
            k0 = pl.multiple_of(kb * blk, blk)
            return kb - 1, all_heads(q, k_ref[pl.ds(k0, blk), :], v_ref[pl.ds(k0, blk), :], False)

        lax.while_loop(cond, body, (qi - 1, m0))
        o = acc_ref[0]
        for h in range(1, SB_HPG):
            o = jnp.where(in_head[h], acc_ref[h], o)
        o_ref[pl.ds(q0, blk), :] = o.astype(BF16)
        return carry

    lax.fori_loop(0, seq // blk, qblock, 0)


def _sb_call(sbqkv, batch, seq):
    t = sbqkv.shape[0]
    ng = SB_W // SB_GROUP
    blk = _pick(seq, (256, 128))
    return pl.pallas_call(
        functools.partial(_sb_kernel, seq=seq, blk=blk),
        grid=(batch, ng),
        in_specs=[pl.BlockSpec((seq, SB_GROUP), lambda b, g: (b, g)),
                  pl.BlockSpec((seq, SB_GROUP), lambda b, g: (b, ng + g)),
                  pl.BlockSpec((seq, SB_GROUP), lambda b, g: (b, 2 * ng + g))],
        out_specs=pl.BlockSpec((seq, SB_GROUP), lambda b, g: (b, g)),
        out_shape=jax.ShapeDtypeStruct((t, SB_W), BF16),
        scratch_shapes=[pltpu.VMEM((SB_HPG, blk, SB_GROUP), F32),
                        pltpu.VMEM((SB_HPG, blk, blk), F32)],
        compiler_params=_params(("arbitrary", "arbitrary")),
        name="sb",
    )(sbqkv, sbqkv, sbqkv)


GDN_PAD = 8


def _gdn_kernel(qkv_ref, z_ref, ba_ref, conv_ref, alog_ref, dtb_ref, ng_ref, o_ref,
                xbuf, u_s, w_s, qd_s, kd_s, qk_s, vn_s, os_, state, *, blk):
    hd = GDN_HEAD_DIM
    w3 = 3 * GDN_W
    nh = GDN_HEADS

    @pl.when(pl.program_id(1) == 0)
    def _():
        state[...] = jnp.zeros_like(state)
        xbuf[0:GDN_PAD, :] = jnp.zeros((GDN_PAD, w3), F32)

    xbuf[GDN_PAD:GDN_PAD + blk, :] = qkv_ref[...]
    cw = conv_ref[...]
    y = xbuf[GDN_PAD:GDN_PAD + blk, :] * cw[CONV_WIDTH - 1:CONV_WIDTH, :]
    for j in range(CONV_WIDTH - 1):
        d = CONV_WIDTH - 1 - j
        y = y + xbuf[GDN_PAD - d:GDN_PAD - d + blk, :] * cw[j:j + 1, :]
    xbuf[0:GDN_PAD, :] = xbuf[blk:blk + GDN_PAD, :]
    y = y * _sigmoid(y)

    ba = ba_ref[...]
    lane = lax.broadcasted_iota(jnp.int32, (blk, LANES), 1)
    g = -jnp.exp(alog_ref[...]) * _softplus(ba + dtb_ref[...])
    g = jnp.where(jnp.logical_and(lane >= nh, lane < 2 * nh), g, 0.0)
    rr = lax.broadcasted_iota(jnp.int32, (blk, blk), 0)
    cc = lax.broadcasted_iota(jnp.int32, (blk, blk), 1)
    same = (rr // CHUNK) == (cc // CHUNK)
    eye = jnp.where(rr == cc, 1.0, 0.0)
    bd_ones = jnp.where(same, 1.0, 0.0)
    bd_incl = jnp.where(cc <= rr, bd_ones, 0.0)
    bd_strict = bd_incl - eye
    gc = _dot_exact_lhs(bd_incl.astype(BF16), g)
    gl = _dot_exact_lhs(bd_ones.astype(BF16), g)
    eg = jnp.exp(gc)
    ek = jnp.exp(gl - gc)
    ec = jnp.exp(gl)
    beta = _sigmoid(ba)
    incl = bd_incl > 0.5

    heads = range(nh)
    xs, p2s, rhs = [], [], []
    for h in heads:
        qh = y[:, h * hd:(h + 1) * hd]
        kh = y[:, GDN_W + h * hd:GDN_W + (h + 1) * hd]
        vh = y[:, 2 * GDN_W + h * hd:2 * GDN_W + (h + 1) * hd]
        q = qh * (lax.rsqrt(jnp.sum(qh * qh, axis=-1, keepdims=True) + EPS) * (hd ** -0.5))
        k = kh * lax.rsqrt(jnp.sum(kh * kh, axis=-1, keepdims=True) + EPS)
        gcol = gc[:, nh + h:nh + h + 1]
        grow = jnp.sum(eye * gcol, axis=0, keepdims=True)
        dm = jnp.exp(jnp.where(incl, gcol - grow, -jnp.inf))
        bcol = beta[:, h:h + 1]
        kb = k * bcol
        k16 = k.astype(BF16)
        a = _dot_nt(kb.astype(BF16), k16) * dm * bd_strict
        egc = eg[:, nh + h:nh + h + 1]
        qk_s[h] = (_dot_nt(q.astype(BF16), k16) * dm).astype(BF16)
        qd_s[h] = (q * egc).astype(BF16)
        kd_s[h] = (k * ek[:, nh + h:nh + h + 1]).astype(BF16)
        rhs.append(_split2(jnp.concatenate([vh * bcol, kb * egc], axis=1)))
        n2 = _split2(-a)
        xs.append(eye - a)
        p2s.append(_split2(_dot_hp2(n2, n2)))
    for it in range(5):
        xs = [xs[h] + _dot_hp2(_split2(xs[h]), p2s[h]) for h in heads]
        if it < 4:
            p2s = [_split2(_dot_hp2(p2s[h], p2s[h])) for h in heads]
    for h in heads:
        sol = _dot_hp2(_split2(xs[h]), rhs[h])
        u_s[h] = sol[:, :hd]
        w_s[h] = sol[:, hd:].astype(BF16)

    vn_s[...] = jnp.zeros_like(vn_s)
    for c in range(blk // CHUNK):
        rows = slice(c * CHUNK, (c + 1) * CHUNK)
        for h in range(nh):
            s = state[h]
            s16 = s.astype(BF16)
            vnew = u_s[h, rows, :] - _dot(w_s[h, rows, :], s16)
            vn16 = vnew.astype(BF16)
            vn_s[h, rows, :] = vn16
            os_[rows, h * hd:(h + 1) * hd] = (_dot(qd_s[h, rows, :], s16)
                                              + _dot(qk_s[h, rows, :], vn_s[h]))
            cd = ec[c * CHUNK:c * CHUNK + 1, nh + h:nh + h + 1]
            state[h] = s * cd + _dot_tn(kd_s[h, rows, :], vn16)

    zg = z_ref[...]
    ng = ng_ref[...]
    for h in range(nh):
        cols = slice(h * hd, (h + 1) * hd)
        oh = _rms(os_[:, cols], ng)
        zh = zg[:, cols]
        o_ref[:, cols] = (oh * (zh * _sigmoid(zh))).astype(BF16)


def _gdn_call(gqkv, gz, gba, conv_w, alog_l, dtb_l, norm_g, batch, seq, blk):
    t = gqkv.shape[0]
    nb = seq // blk
    row = lambda b, i: (b * nb + i, 0)
    fixed = lambda b, i: (0, 0)
    w3 = 3 * GDN_W
    nh, hd = GDN_HEADS, GDN_HEAD_DIM
    return pl.pallas_call(
        functools.partial(_gdn_kernel, blk=blk),
        grid=(batch, nb),
        in_specs=[pl.BlockSpec((blk, w3), row),
                  pl.BlockSpec((blk, GDN_W), row),
                  pl.BlockSpec((blk, LANES), row),
                  pl.BlockSpec((CONV_WIDTH, w3), fixed),
                  pl.BlockSpec((1, LANES), fixed),
                  pl.BlockSpec((1, LANES), fixed),
                  pl.BlockSpec((1, hd), fixed)],
        out_specs=pl.BlockSpec((blk, GDN_W), row),
        out_shape=jax.ShapeDtypeStruct((t, GDN_W), BF16),
        scratch_shapes=[pltpu.VMEM((blk + GDN_PAD, w3), F32),
                        pltpu.VMEM((nh, blk, hd), F32),
                        pltpu.VMEM((nh, blk, hd), BF16),
                        pltpu.VMEM((nh, blk, hd), BF16),
                        pltpu.VMEM((nh, blk, hd), BF16),
                        pltpu.VMEM((nh, blk, blk), BF16),
                        pltpu.VMEM((nh, blk, hd), BF16),
                        pltpu.VMEM((blk, GDN_W), F32),
                        pltpu.VMEM((nh, hd, hd), F32)],
        compiler_params=_params(("arbitrary", "arbitrary")),
        name="gdn",
    )(gqkv, gz, gba, conv_w, alog_l, dtb_l, norm_g)


def _merge_kernel(x_ref, osb_ref, og_ref, gt_ref, kv_ref, wsb_ref, wg_ref, wmix_ref,
                  gcr_ref, wcq_ref, wco_ref, gpe_ref, x2_ref, xpt_ref, *, d):
    gates = gt_ref[...].astype(F32)
    merged = (_sigmoid(gates[:, :d]) * _dot(osb_ref[...], wsb_ref[...])
              + _sigmoid(gates[:, d:]) * _dot(og_ref[...], wg_ref[...]))
    x1 = x_ref[...] + _dot(merged.astype(BF16), wmix_ref[...])
    hn = _rms(x1, gcr_ref[...]).astype(BF16)
    q = _dot(hn, wcq_ref[...])
    xd = d // X_HEADS
    kv = kv_ref[...]
    outs = []
    for h in range(X_HEADS):
        qh = q[:, h * xd:(h + 1) * xd].astype(BF16)
        kh = kv[:, h * xd:(h + 1) * xd]
        vh = kv[:, d + h * xd:d + (h + 1) * xd]
        s = _dot_nt(qh, kh) * (xd ** -0.5)
        e = jnp.exp(s - jnp.max(s, axis=-1, keepdims=True))
        p = e / jnp.sum(e, axis=-1, keepdims=True)
        outs.append(_dot(p.astype(BF16), vh))
    o = jnp.concatenate(outs, axis=-1).astype(BF16)
    x2 = x1 + _dot(o, wco_ref[...])
    x2_ref[...] = x2
    xpt_ref[...] = _rms(x2, gpe_ref[...]).T.astype(BF16)


def _merge_call(x2d, osb, og, gates, kv, wsb, wg, wmix, g_cross, wcq, wco, g_peer, seq, n_mem, tm):
    t, d = x2d.shape
    per_b = seq // tm
    row = lambda i: (i, 0)
    fixed = lambda i: (0, 0)
    return pl.pallas_call(
        functools.partial(_merge_kernel, d=d),
        grid=(t // tm,),
        in_specs=[pl.BlockSpec((tm, d), row),
                  pl.BlockSpec((tm, SB_W), row),
                  pl.BlockSpec((tm, GDN_W), row),
                  pl.BlockSpec((tm, 2 * d), row),
                  pl.BlockSpec((n_mem, 2 * d), lambda i: (i // per_b, 0)),
                  pl.BlockSpec((SB_W, d), fixed),
                  pl.BlockSpec((GDN_W, d), fixed),
                  pl.BlockSpec((d, d), fixed),
                  pl.BlockSpec((1, d), fixed),
                  pl.BlockSpec((d, d), fixed),
                  pl.BlockSpec((d, d), fixed),
                  pl.BlockSpec((1, d), fixed)],
        out_specs=[pl.BlockSpec((tm, d), row), pl.BlockSpec((d, tm), lambda i: (0, i))],
        out_shape=[jax.ShapeDtypeStruct((t, d), F32), jax.ShapeDtypeStruct((d, t), BF16)],
        compiler_params=_params(("arbitrary",)),
        name="merge",
    )(x2d, osb, og, gates, kv, wsb, wg, wmix, g_cross, wcq, wco, g_peer)


N_HP = 2 * PEER_HEADS
N_TOP = PEER_TOPK + 1
PAIR_ROWS = [min(N_TOP, N_TOP // (a + 1)) for a in range(N_TOP)]
N_SINGLE = sum(1 for n in PAIR_ROWS if n == 1)
N_CAND = -(-sum(PAIR_ROWS) // 8) * 8
TOP_PAD = -(-N_TOP // 8) * 8


def _sort_network(n):
    pairs = []
    p = 1
    while p < n:
        k = p
        while k >= 1:
            for j in range(k % p, n - k, 2 * k):
                for i in range(min(k, n - j - k)):
                    if (i + j) // (2 * p) == (i + j + k) // (2 * p):
                        pairs.append((i + j, i + j + k))
            k //= 2
        p *= 2
    return pairs


def _peersel_kernel(xpt_ref, wpqt_ref, sk_ref, r2_ref, b2_ref, n1_ref, a1_ref,
                    qps, sall, tops, cand, best):
    qpt = _dot(wpqt_ref[...], xpt_ref[...])
    for hp in range(N_HP):
        qps[hp] = qpt[hp * PEER_KEYS:(hp + 1) * PEER_KEYS, :].astype(BF16)

    def top_rounds(srcs, dsts, rounds):
        def rnd(r, carry):
            for src_ref, dst_ref in zip(srcs, dsts):
                c = src_ref[...]
                m = jnp.max(c, axis=0, keepdims=True)
                dst_ref[pl.ds(r, 1), :] = m
                src_ref[...] = jnp.where(c == m, -jnp.inf, c)
            return carry
        lax.fori_loop(0, rounds, rnd, 0)

    def top_values(s, dst_ref):
        sub = s.shape[0] // SUBLANES
        v = [s[k * SUBLANES:(k + 1) * SUBLANES, :] for k in range(sub)]
        for a, b in _sort_network(sub):
            v[a], v[b] = jnp.maximum(v[a], v[b]), jnp.minimum(v[a], v[b])
        for r in range(N_TOP):
            m = jnp.max(v[0], axis=0, keepdims=True)
            dst_ref[r:r + 1, :] = m
            hit = v[0] == m
            for k in range(min(sub, N_TOP - r - 1)):
                nxt = v[k + 1] if k + 1 < sub else jnp.full_like(v[k], -jnp.inf)
                v[k] = jnp.where(hit, nxt, v[k])

    def head_scores(h, carry):
        for p in range(2):
            s = _dot(sk_ref[2 * h + p], qps[2 * h + p])
            sall[2 * h + p] = s
            top_values(s, tops.at[2 * h + p])
        return carry

    lax.fori_loop(0, PEER_HEADS, head_scores, 0)

    n_multi = N_TOP - N_SINGLE
    cand[...] = jnp.full(cand.shape, -jnp.inf, F32)

    def head(h, carry):
        xs = tops[2 * h, 0:N_TOP, :]
        ys = tops[2 * h + 1, 0:N_TOP, :]
        row = 0
        for a in range(n_multi):
            cand[row:row + PAIR_ROWS[a], :] = xs[a:a + 1, :] + ys[0:PAIR_ROWS[a], :]
            row += PAIR_ROWS[a]
        cand[row:row + N_SINGLE, :] = xs[n_multi:N_TOP, :] + ys[0:1, :]
        top_rounds([cand], [best], N_TOP)
        bs = best[0:N_TOP, :]
        top = bs[0:PEER_TOPK, :]
        zsum = jnp.sum(jnp.exp(top - top[0:1, :]), axis=0, keepdims=True)
        theta = 0.5 * (bs[PEER_TOPK - 1:PEER_TOPK, :] + bs[PEER_TOPK:PEER_TOPK + 1, :])
        s1 = sall[2 * h]
        s2 = sall[2 * h + 1]
        thr = theta - s1
        rank = jnp.zeros_like(s2)
        count = jnp.zeros_like(s1)
        for b in range(N_TOP):
            yb = ys[b:b + 1, :]
            rank = jnp.where(yb > s2, b + 1.0, rank)
            count = jnp.where(yb >= thr, b + 1.0, count)
        rows = pl.ds(pl.multiple_of(h * PEER_KEYS, PEER_KEYS), PEER_KEYS)
        rows16 = pl.ds(pl.multiple_of(h * (PEER_KEYS // 2), PEER_KEYS // 2), PEER_KEYS // 2)
        r2_ref[rows16, :] = pltpu.bitcast(rank.astype(BF16), jnp.int32)
        b2_ref[rows16, :] = pltpu.bitcast(jnp.exp(s2 - ys[0:1, :]).astype(BF16), jnp.int32)
        n1_ref[rows, :] = count
        a1_ref[rows, :] = jnp.exp(s1 - xs[0:1, :]) / zsum
        return carry

    lax.fori_loop(0, PEER_HEADS, head, 0)


def _peersel_call(xpt, wpqt, sk, tm):
    d, t = xpt.shape
    nq = wpqt.shape[0]
    kd = sk.shape[-1]
    out32 = jax.ShapeDtypeStruct((PEER_HEADS * PEER_KEYS, t), F32)
    out16 = jax.ShapeDtypeStruct((PEER_HEADS * PEER_KEYS // 2, t), jnp.int32)
    ospec = pl.BlockSpec((PEER_HEADS * PEER_KEYS, tm), lambda i: (0, i))
    ospec16 = pl.BlockSpec((PEER_HEADS * PEER_KEYS // 2, tm), lambda i: (0, i))
    return pl.pallas_call(
        _peersel_kernel,
        grid=(t // tm,),
        in_specs=[pl.BlockSpec((d, tm), lambda i: (0, i)),
                  pl.BlockSpec((nq, d), lambda i: (0, 0)),
                  pl.BlockSpec((N_HP, PEER_KEYS, kd), lambda i: (0, 0, 0))],
        out_specs=[ospec16, ospec16, ospec, ospec],
        out_shape=[out16, out16, out32, out32],
        scratch_shapes=[pltpu.VMEM((N_HP, kd, tm), BF16),
                        pltpu.VMEM((N_HP, PEER_KEYS, tm), F32),
                        pltpu.VMEM((N_HP, TOP_PAD, tm), F32),
                        pltpu.VMEM((N_CAND, tm), F32),
                        pltpu.VMEM((TOP_PAD, tm), F32)],
        compiler_params=_params(("arbitrary",)),
        name="peersel",
    )(xpt, wpqt, sk)


PEER_EBLK = 8 * PEER_KEYS
PEER_GRP = 2 * PEER_KEYS
PEER_RC = 16


def _gelu(x):
    return 0.5 * x * (1.0 + lax.erf(x * 0.7071067811865476))


def _peer_kernel(xpt_ref, u_ref, vt_ref, r2_ref, b2_ref, n1_ref, a1_ref, x2_ref, gf_ref, o_ref,
                 ht, at, acc):
    e = pl.program_id(1)

    @pl.when(e == 0)
    def _():
        acc[...] = jnp.zeros_like(acc)

    n_i = PEER_EBLK // PEER_KEYS
    n_grp = PEER_EBLK // PEER_GRP

    def scores(g):
        rows = slice(g * PEER_GRP, (g + 1) * PEER_GRP)
        ht[rows, :] = _dot(u_ref[rows, :], xpt_ref[...])

    def gates(sub):
        i = n_i * e + sub
        for lt in range(ht.shape[1] // LANES):
            tl = slice(lt * LANES, (lt + 1) * LANES)
            def row16(ref, h):
                row = ref[pl.ds(h * PEER_KEYS + i, 1), :][:, tl]
                return jnp.broadcast_to(row, (PEER_RC, LANES)).astype(BF16)
            cnt = [row16(n1_ref, h) for h in range(PEER_HEADS)]
            cf = [row16(a1_ref, h) for h in range(PEER_HEADS)]
            zero = jnp.zeros((PEER_RC, LANES), BF16)
            for r in range(PEER_KEYS // PEER_RC):
                j0 = r * PEER_RC
                w = None
                for h in range(PEER_HEADS):
                    jr = slice((h * PEER_KEYS + j0) // 2, (h * PEER_KEYS + j0 + PEER_RC) // 2)
                    r2 = pltpu.bitcast(r2_ref[jr, tl], BF16)
                    b2 = pltpu.bitcast(b2_ref[jr, tl], BF16)
                    wh = jnp.where(r2 < cnt[h], b2 * cf[h], zero)
                    w = wh if w is None else w + wh
                row0 = sub * PEER_KEYS + j0
                at[row0:row0 + PEER_RC, tl] = _gelu(ht[row0:row0 + PEER_RC, tl]).astype(BF16) * w

    scores(0)
    for g in range(n_grp):
        if g + 1 < n_grp:
            scores(g + 1)
        for sub in range(g * PEER_GRP // PEER_KEYS, (g + 1) * PEER_GRP // PEER_KEYS):
            gates(sub)
        if g % 2 == 1:
            rows = slice((g - 1) * PEER_GRP, (g + 1) * PEER_GRP)
            acc[...] += _dot(vt_ref[:, rows], at[rows, :])

    @pl.when(e == pl.num_programs(1) - 1)
    def _():
        x3 = x2_ref[...] + acc[...].T
        o_ref[...] = _rms(x3, gf_ref[...])


def _peer_call(xpt, u16, vt16, r2, b2, n1, a1, x2, g_final, tb):
    d, t = xpt.shape
    n_exp = u16.shape[0]
    tok = lambda i, e: (i, 0)
    sel = pl.BlockSpec((PEER_HEADS * PEER_KEYS, tb), lambda i, e: (0, i))
    sel16 = pl.BlockSpec((PEER_HEADS * PEER_KEYS // 2, tb), lambda i, e: (0, i))
    return pl.pallas_call(
        _peer_kernel,
        grid=(t // tb, n_exp // PEER_EBLK),
        in_specs=[pl.BlockSpec((d, tb), lambda i, e: (0, i)),
                  pl.BlockSpec((PEER_EBLK, d), lambda i, e: (e, 0)),
                  pl.BlockSpec((d, PEER_EBLK), lambda i, e: (0, e)),
                  sel16, sel16, sel, sel,
                  pl.BlockSpec((tb, d), tok),
                  pl.BlockSpec((1, d), lambda i, e: (0, 0))],
        out_specs=pl.BlockSpec((tb, d), tok),
        out_shape=jax.ShapeDtypeStruct((t, d), F32),
        scratch_shapes=[pltpu.VMEM((PEER_EBLK, tb), F32),
                        pltpu.VMEM((PEER_EBLK, tb), BF16),
                        pltpu.VMEM((d, tb), F32)],
        compiler_params=_params(("arbitrary", "arbitrary")),
        name="peer",
    )(xpt, u16, vt16, r2, b2, n1, a1, x2, g_final)


def _regroup_w_in(w_in):
    o_ba = 3 * SB_W + 4 * GDN_W
    ba = jnp.pad(w_in[:, o_ba:o_ba + 2 * GDN_HEADS], ((0, 0), (0, LANES - 2 * GDN_HEADS)))
    return jnp.concatenate([w_in[:, :o_ba], w_in[:, o_ba + 2 * GDN_HEADS:], ba], axis=1).astype(BF16)


def _layer(x2d, kv, batch, seq, n_mem, g_mix, w_in, gdn_conv, gdn_a_log, gdn_dt_bias, gdn_norm_g,
           w_sb_up, w_gdn_up, w_mix_out, g_cross, w_cq, w_co, g_peer, w_pq, peer_subkeys,
           peer_u, peer_v, g_out):
    t, d = x2d.shape
    tm = _pick(seq, (256, 128))
    sbqkv, gqkv, gz, gates, gba = _inproj_call(x2d, g_mix.reshape(1, d), _regroup_w_in(w_in), tm)
    o_sb = _sb_call(sbqkv, batch, seq)
    lane_pad = lambda v: jnp.pad(v.astype(F32), (GDN_HEADS, LANES - 2 * GDN_HEADS)).reshape(1, LANES)
    o_g = _gdn_call(gqkv, gz, gba, gdn_conv, lane_pad(gdn_a_log), lane_pad(gdn_dt_bias),
                    gdn_norm_g.reshape(1, GDN_HEAD_DIM), batch, seq, _pick(seq, (256, 128, 64)))
    x2, xpt = _merge_call(x2d, o_sb, o_g, gates, kv, w_sb_up.astype(BF16), w_gdn_up.astype(BF16),
                          w_mix_out.astype(BF16), g_cross.reshape(1, d), w_cq.astype(BF16),
                          w_co.astype(BF16), g_peer.reshape(1, d), seq, n_mem, tm)
    sk = peer_subkeys.reshape(N_HP, PEER_KEYS, -1).astype(BF16)
    r2, b2, n1, a1 = _peersel_call(xpt, w_pq.T.astype(BF16), sk, tm)
    return _peer_call(xpt, peer_u.astype(BF16), peer_v.T.astype(BF16), r2, b2, n1, a1, x2,
                      g_out.reshape(1, d), _pick(t, (512, 256, 128)))


def kernel(x, mem, g_mix, w_in, gdn_conv, gdn_a_log, gdn_dt_bias, gdn_norm_g, w_sb_up, w_gdn_up,
           w_mix_out, g_cross, g_mem, w_cq, w_ckv, w_co, g_peer, w_pq, peer_subkeys, peer_u, peer_v,
           g_final):
    batch, seq, d = x.shape
    n_mem = mem.shape[1]
    depth = g_mix.shape[0]
    assert depth == 1, "the final RMSNorm is fused into the last layer's PEER kernel"
    x2d = x.reshape(batch * seq, d)
    mem2d = mem.reshape(batch * n_mem, d)
    l = 0
    kv = _kv_call(mem2d, g_mem[l].reshape(1, d), w_ckv[l].astype(BF16), n_mem)
    out = _layer(x2d, kv, batch, seq, n_mem, g_mix[l], w_in[l], gdn_conv[l], gdn_a_log[l],
                 gdn_dt_bias[l], gdn_norm_g[l], w_sb_up[l], w_gdn_up[l], w_mix_out[l], g_cross[l],
                 w_cq[l], w_co[l], g_peer[l], w_pq[l], peer_subkeys[l], peer_u[l], peer_v[l], g_final)
    return out.reshape(batch, seq, d)
```

```python
import functools

import jax
import jax.numpy as jnp
from jax import lax
from jax.experimental import pallas as pl
from jax.experimental.pallas import tpu as pltpu

F32 = jnp.float32
BF16 = jnp.bfloat16

EPS = 1e-6
SB_HEADS = 8
SB_HEAD_DIM = 64
GDN_HEADS = 4
GDN_HEAD_DIM = 128
CONV_WIDTH = 4
CHUNK = 64
X_HEADS = 4
PEER_HEADS = 8
PEER_KEYS = 128
PEER_TOPK = 16

LANES = 128
SUBLANES = 8
VMEM_LIMIT = 56 * 1024 * 1024
SB_SKIP_LOG = -104.0


def _dot(a, b):
    return jnp.dot(a, b, preferred_element_type=F32)


def _dot_nt(a, b):
    return lax.dot_general(a, b, (((1,), (1,)), ((), ())), preferred_element_type=F32)


def _dot_tn(a, b):
    return lax.dot_general(a, b, (((0,), (0,)), ((), ())), preferred_element_type=F32)


def _split3(a):
    hi = a.astype(BF16)
    r1 = a - hi.astype(F32)
    mid = r1.astype(BF16)
    lo = (r1 - mid.astype(F32)).astype(BF16)
    return hi, mid, lo


def _dot_exact_lhs(m_bf16, b):
    hi, mid, lo = _split3(b)
    return _dot(m_bf16, hi) + _dot(m_bf16, mid) + _dot(m_bf16, lo)


def _split2(a):
    hi = a.astype(BF16)
    return hi, (a - hi.astype(F32)).astype(BF16)


def _dot_hp2(a2, b2):
    return _dot(a2[0], b2[0]) + _dot(a2[0], b2[1]) + _dot(a2[1], b2[0])


def _rms(x, g):
    return x * lax.rsqrt(jnp.mean(x * x, axis=-1, keepdims=True) + EPS) * g


def _softplus(x):
    return jnp.maximum(x, 0.0) + jnp.log1p(jnp.exp(-jnp.abs(x)))


def _sigmoid(x):
    return 1.0 / (1.0 + jnp.exp(-x))


def _params(sem):
    return pltpu.CompilerParams(dimension_semantics=sem, vmem_limit_bytes=VMEM_LIMIT)


def _pick(n, pref):
    for c in pref:
        if n % c == 0:
            return c
    return n


def _kv_kernel(mem_ref, g_ref, w_ref, o_ref):
    mn = _rms(mem_ref[...], g_ref[...]).astype(BF16)
    o_ref[...] = _dot(mn, w_ref[...]).astype(BF16)


def _kv_call(mem2d, g_mem, w_ckv, n_mem):
    rows, d = mem2d.shape
    return pl.pallas_call(
        _kv_kernel,
        grid=(rows // n_mem,),
        in_specs=[pl.BlockSpec((n_mem, d), lambda i: (i, 0)),
                  pl.BlockSpec((1, d), lambda i: (0, 0)),
                  pl.BlockSpec((d, 2 * d), lambda i: (0, 0))],
        out_specs=pl.BlockSpec((n_mem, 2 * d), lambda i: (i, 0)),
        out_shape=jax.ShapeDtypeStruct((rows, 2 * d), BF16),
        compiler_params=_params(("arbitrary",)),
        name="kv",
    )(mem2d, g_mem, w_ckv)


SB_W = SB_HEADS * SB_HEAD_DIM
GDN_W = GDN_HEADS * GDN_HEAD_DIM
C_SB = (0, 3 * SB_W)
C_GQ = (C_SB[1], C_SB[1] + 3 * GDN_W)
C_GZ = (C_GQ[1], C_GQ[1] + GDN_W)


def _inproj_kernel(x_ref, g_ref, w_ref, sb_ref, gq_ref, gz_ref, gt_ref, ba_ref, *, d):
    xn = _rms(x_ref[...], g_ref[...]).astype(BF16)
    c_gt = (C_GZ[1], C_GZ[1] + 2 * d)
    c_ba = (c_gt[1], c_gt[1] + LANES)

    def mm(c):
        return _dot(xn, w_ref[:, c[0]:c[1]])

    sb_ref[...] = mm(C_SB).astype(BF16)
    gq_ref[...] = mm(C_GQ)
    gz_ref[...] = mm(C_GZ)
    gt_ref[...] = mm(c_gt).astype(BF16)
    ba_ref[...] = mm(c_ba)


def _inproj_call(x2d, g_mix, w_p, tm):
    t, d = x2d.shape
    n = w_p.shape[1]
    row = lambda i: (i, 0)
    fixed = lambda i: (0, 0)
    return pl.pallas_call(
        functools.partial(_inproj_kernel, d=d),
        grid=(t // tm,),
        in_specs=[pl.BlockSpec((tm, d), row),
                  pl.BlockSpec((1, d), fixed),
                  pl.BlockSpec((d, n), fixed)],
        out_specs=[pl.BlockSpec((tm, 3 * SB_W), row),
                   pl.BlockSpec((tm, 3 * GDN_W), row),
                   pl.BlockSpec((tm, GDN_W), row),
                   pl.BlockSpec((tm, 2 * d), row),
                   pl.BlockSpec((tm, LANES), row)],
        out_shape=[jax.ShapeDtypeStruct((t, 3 * SB_W), BF16),
                   jax.ShapeDtypeStruct((t, 3 * GDN_W), F32),
                   jax.ShapeDtypeStruct((t, GDN_W), F32),
                   jax.ShapeDtypeStruct((t, 2 * d), BF16),
                   jax.ShapeDtypeStruct((t, LANES), F32)],
        compiler_params=_params(("arbitrary",)),
        name="inproj",
    )(x2d, g_mix, w_p)


SB_GROUP = 2 * LANES
SB_HPG = SB_GROUP // SB_HEAD_DIM


def _sb_kernel(q_ref, k_ref, v_ref, o_ref, acc_ref, c_ref, *, seq, blk):
    scale = SB_HEAD_DIM ** -0.5
    lane = lax.broadcasted_iota(jnp.int32, (blk, SB_GROUP), 1)
    in_head = [jnp.logical_and(lane >= h * SB_HEAD_DIM, lane < (h + 1) * SB_HEAD_DIM)
               for h in range(SB_HPG)]
    r_i = lax.broadcasted_iota(jnp.int32, (blk, blk), 0)
    c_i = lax.broadcasted_iota(jnp.int32, (blk, blk), 1)
    causal = c_i < r_i
    jj = lax.broadcasted_iota(jnp.int32, (blk, 2 * blk), 0)
    ss = lax.broadcasted_iota(jnp.int32, (blk, 2 * blk), 1)
    m_ext = jnp.where(ss >= blk, 1.0, jnp.where(jj > ss, 1.0, 0.0)).astype(BF16)

    def all_heads(q, kblk, vblk, masked):
        zero = jnp.zeros_like(q)
        hs = range(SB_HPG)
        z = [_dot_nt(jnp.where(in_head[h], q, zero), kblk) * scale for h in hs]
        sp = [_softplus(z[h]) for h in hs]
        lk = [-sp[h] for h in hs]
        if masked:
            lk = [jnp.where(causal, lk[h], 0.0) for h in hs]
        parts = [_split2(lk[h]) for h in hs]
        ext = [_dot(parts[h][0], m_ext) + _dot(parts[h][1], m_ext) for h in hs]
        cm = None
        for h in hs:
            c = c_ref[h]
            a = jnp.exp((z[h] - sp[h]) + ext[h][:, :blk] + c)
            if masked:
                a = jnp.where(causal, a, 0.0)
            acc_ref[h] += _dot(a.astype(BF16), vblk)
            c_new = c + ext[h][:, blk:]
            c_ref[h] = c_new
            cm = c_new if cm is None else jnp.maximum(cm, c_new)
        return jnp.max(cm)

    def qblock(qi, carry):
        q0 = pl.multiple_of(qi * blk, blk)
        q = q_ref[pl.ds(q0, blk), :]
        acc_ref[...] = jnp.zeros_like(acc_ref)
        c_ref[...] = jnp.zeros_like(c_ref)
        m0 = all_heads(q, k_ref[pl.ds(q0, blk), :], v_ref[pl.ds(q0, blk), :], True)

        def cond(st):
            kb, cm = st
            return jnp.logical_and(kb >= 0, cm > SB_SKIP_LOG)

        def body(st):
            kb, _ = st
            k0 = pl.multiple_of(kb * blk, blk)
            return kb - 1, all_heads(q, k_ref[pl.ds(k0, blk), :], v_ref[pl.ds(k0, blk), :], False)

        lax.while_loop(cond, body, (qi - 1, m0))
        o = acc_ref[0]
        for h in range(1, SB_HPG):
            o = jnp.where(in_head[h], acc_ref[h], o)
        o_ref[pl.ds(q0, blk), :] = o.astype(BF16)
        return carry

    lax.fori_loop(0, seq // blk, qblock, 0)


def _sb_call(sbqkv, batch, seq):
    t = sbqkv.shape[0]
    ng = SB_W // SB_GROUP
    blk = _pick(seq, (256, 128))
    return pl.pallas_call(
        functools.partial(_sb_kernel, seq=seq, blk=blk),
        grid=(batch, ng),
        in_specs=[pl.BlockSpec((seq, SB_GROUP), lambda b, g: (b, g)),
                  pl.BlockSpec((seq, SB_GROUP), lambda b, g: (b, ng + g)),
                  pl.BlockSpec((seq, SB_GROUP), lambda b, g: (b, 2 * ng + g))],
        out_specs=pl.BlockSpec((seq, SB_GROUP), lambda b, g: (b, g)),
        out_shape=jax.ShapeDtypeStruct((t, SB_W), BF16),
        scratch_shapes=[pltpu.VMEM((SB_HPG, blk, SB_GROUP), F32),
                        pltpu.VMEM((SB_HPG, blk, blk), F32)],
        compiler_params=_params(("arbitrary", "arbitrary")),
        name="sb",
    )(sbqkv, sbqkv, sbqkv)


GDN_PAD = 8
GDN_BASE = 8


def _gdn_kernel(qkv_ref, z_ref, ba_ref, conv_ref, alog_ref, dtb_ref, ng_ref, o_ref,
                xbuf, u_s, w_s, qd_s, kd_s, qk_s, vn_s, os_, state, *, blk):
    hd = GDN_HEAD_DIM
    w3 = 3 * GDN_W
    nh = GDN_HEADS

    @pl.when(pl.program_id(1) == 0)
    def _():
        state[...] = jnp.zeros_like(state)
        xbuf[0:GDN_PAD, :] = jnp.zeros((GDN_PAD, w3), F32)

    xbuf[GDN_PAD:GDN_PAD + blk, :] = qkv_ref[...]
    cw = conv_ref[...]
    y = xbuf[GDN_PAD:GDN_PAD + blk, :] * cw[CONV_WIDTH - 1:CONV_WIDTH, :]
    for j in range(CONV_WIDTH - 1):
        d = CONV_WIDTH - 1 - j
        y = y + xbuf[GDN_PAD - d:GDN_PAD - d + blk, :] * cw[j:j + 1, :]
    xbuf[0:GDN_PAD, :] = xbuf[blk:blk + GDN_PAD, :]
    y = y * _sigmoid(y)

    ba = ba_ref[...]
    lane = lax.broadcasted_iota(jnp.int32, (blk, LANES), 1)
    g = -jnp.exp(alog_ref[...]) * _softplus(ba + dtb_ref[...])
    g = jnp.where(jnp.logical_and(lane >= nh, lane < 2 * nh), g, 0.0)
    rr = lax.broadcasted_iota(jnp.int32, (blk, blk), 0)
    cc = lax.broadcasted_iota(jnp.int32, (blk, blk), 1)
    same = (rr // CHUNK) == (cc // CHUNK)
    eye = jnp.where(rr == cc, 1.0, 0.0)
    bd_ones = jnp.where(same, 1.0, 0.0)
    bd_incl = jnp.where(cc <= rr, bd_ones, 0.0)
    bd_strict = bd_incl - eye
    gc = _dot_exact_lhs(bd_incl.astype(BF16), g)
    gl = _dot_exact_lhs(bd_ones.astype(BF16), g)
    eg = jnp.exp(gc)
    ek = jnp.exp(gl - gc)
    ec = jnp.exp(gl)
    beta = _sigmoid(ba)
    incl = bd_incl > 0.5

    heads = range(nh)
    a_s, rhs = [], []
    for h in heads:
        qh = y[:, h * hd:(h + 1) * hd]
        kh = y[:, GDN_W + h * hd:GDN_W + (h + 1) * hd]
        vh = y[:, 2 * GDN_W + h * hd:2 * GDN_W + (h + 1) * hd]
        q = qh * (lax.rsqrt(jnp.sum(qh * qh, axis=-1, keepdims=True) + EPS) * (hd ** -0.5))
        k = kh * lax.rsqrt(jnp.sum(kh * kh, axis=-1, keepdims=True) + EPS)
        gcol = gc[:, nh + h:nh + h + 1]
        grow = jnp.sum(eye * gcol, axis=0, keepdims=True)
        dm = jnp.exp(jnp.where(incl, gcol - grow, -jnp.inf))
        bcol = beta[:, h:h + 1]
        kb = k * bcol
        k16 = k.astype(BF16)
        a = _dot_nt(kb.astype(BF16), k16) * dm * bd_strict
        egc = eg[:, nh + h:nh + h + 1]
        qk_s[h] = (_dot_nt(q.astype(BF16), k16) * dm).astype(BF16)
        qd_s[h] = (q * egc).astype(BF16)
        kd_s[h] = (k * ek[:, nh + h:nh + h + 1]).astype(BF16)
        rhs.append(_split2(jnp.concatenate([vh * bcol, kb * egc], axis=1)))
        a_s.append(a)
    def mm(p, q):
        return _dot(p.astype(BF16), q.astype(BF16))

    in_base = jnp.where(rr // GDN_BASE == cc // GDN_BASE, 1.0, 0.0)
    xs = []
    for h in heads:
        n = -(a_s[h] * in_base)
        x = eye + n
        p = mm(n, n)
        w = 2
        while w < GDN_BASE:
            x = x + mm(x, p)
            w *= 2
            if w < GDN_BASE:
                p = mm(p, p)
        xs.append(x)
    w = GDN_BASE
    while w < CHUNK:
        m_off = jnp.where(jnp.logical_and(rr // w != cc // w, rr // (2 * w) == cc // (2 * w)), 1.0, 0.0)
        xs = [xs[h] - mm(mm(xs[h], a_s[h] * m_off), xs[h]) for h in heads]
        w *= 2
    for h in heads:
        sol = _dot_hp2(_split2(xs[h]), rhs[h])
        u_s[h] = sol[:, :hd]
        w_s[h] = sol[:, hd:].astype(BF16)

    vn_s[...] = jnp.zeros_like(vn_s)
    for c in range(blk // CHUNK):
        rows = slice(c * CHUNK, (c + 1) * CHUNK)
        for h in range(nh):
            s = state[h]
            s16 = s.astype(BF16)
            vnew = u_s[h, rows, :] - _dot(w_s[h, rows, :], s16)
            vn16 = vnew.astype(BF16)
            vn_s[h, rows, :] = vn16
            os_[rows, h * hd:(h + 1) * hd] = (_dot(qd_s[h, rows, :], s16)
                                              + _dot(qk_s[h, rows, :], vn_s[h]))
            cd = ec[c * CHUNK:c * CHUNK + 1, nh + h:nh + h + 1]
            state[h] = s * cd + _dot_tn(kd_s[h, rows, :], vn16)

    zg = z_ref[...]
    ng = ng_ref[...]
    for h in range(nh):
        cols = slice(h * hd, (h + 1) * hd)
        oh = _rms(os_[:, cols], ng)
        zh = zg[:, cols]
        o_ref[:, cols] = (oh * (zh * _sigmoid(zh))).astype(BF16)


def _gdn_call(gqkv, gz, gba, conv_w, alog_l, dtb_l, norm_g, batch, seq, blk):
    t = gqkv.shape[0]
    nb = seq // blk
    row = lambda b, i: (b * nb + i, 0)
    fixed = lambda b, i: (0, 0)
    w3 = 3 * GDN_W
    nh, hd = GDN_HEADS, GDN_HEAD_DIM
    return pl.pallas_call(
        functools.partial(_gdn_kernel, blk=blk),
        grid=(batch, nb),
        in_specs=[pl.BlockSpec((blk, w3), row),
                  pl.BlockSpec((blk, GDN_W), row),
                  pl.BlockSpec((blk, LANES), row),
                  pl.BlockSpec((CONV_WIDTH, w3), fixed),
                  pl.BlockSpec((1, LANES), fixed),
                  pl.BlockSpec((1, LANES), fixed),
                  pl.BlockSpec((1, hd), fixed)],
        out_specs=pl.BlockSpec((blk, GDN_W), row),
        out_shape=jax.ShapeDtypeStruct((t, GDN_W), BF16),
        scratch_shapes=[pltpu.VMEM((blk + GDN_PAD, w3), F32),
                        pltpu.VMEM((nh, blk, hd), F32),
                        pltpu.VMEM((nh, blk, hd), BF16),
                        pltpu.VMEM((nh, blk, hd), BF16),
                        pltpu.VMEM((nh, blk, hd), BF16),
                        pltpu.VMEM((nh, blk, blk), BF16),
                        pltpu.VMEM((nh, blk, hd), BF16),
                        pltpu.VMEM((blk, GDN_W), F32),
                        pltpu.VMEM((nh, hd, hd), F32)],
        compiler_params=_params(("arbitrary", "arbitrary")),
        name="gdn",
    )(gqkv, gz, gba, conv_w, alog_l, dtb_l, norm_g)


def _merge_kernel(x_ref, osb_ref, og_ref, gt_ref, kv_ref, wsb_ref, wg_ref, wmix_ref,
                  gcr_ref, wcq_ref, wco_ref, gpe_ref, x2_ref, xpt_ref, *, d):
    gates = gt_ref[...].astype(F32)
    merged = (_sigmoid(gates[:, :d]) * _dot(osb_ref[...], wsb_ref[...])
              + _sigmoid(gates[:, d:]) * _dot(og_ref[...], wg_ref[...]))
    x1 = x_ref[...] + _dot(merged.astype(BF16), wmix_ref[...])
    hn = _rms(x1, gcr_ref[...]).astype(BF16)
    q = _dot(hn, wcq_ref[...])
    xd = d // X_HEADS
    kv = kv_ref[...]
    hs = range(X_HEADS)
    sc = [_dot_nt(q[:, h * xd:(h + 1) * xd].astype(BF16), kv[:, h * xd:(h + 1) * xd]) * (xd ** -0.5)
          for h in hs]
    ex = [jnp.exp(sc[h] - jnp.max(sc[h], axis=-1, keepdims=True)) for h in hs]
    pr = [(ex[h] / jnp.sum(ex[h], axis=-1, keepdims=True)).astype(BF16) for h in hs]
    outs = [_dot(pr[h], kv[:, d + h * xd:d + (h + 1) * xd]) for h in hs]
    o = jnp.concatenate(outs, axis=-1).astype(BF16)
    x2 = x1 + _dot(o, wco_ref[...])
    x2_ref[...] = x2
    xpt_ref[...] = _rms(x2, gpe_ref[...]).T.astype(BF16)


def _merge_call(x2d, osb, og, gates, kv, wsb, wg, wmix, g_cross, wcq, wco, g_peer, seq, n_mem, tm):
    t, d = x2d.shape
    per_b = seq // tm
    row = lambda i: (i, 0)
    fixed = lambda i: (0, 0)
    return pl.pallas_call(
        functools.partial(_merge_kernel, d=d),
        grid=(t // tm,),
        in_specs=[pl.BlockSpec((tm, d), row),
                  pl.BlockSpec((tm, SB_W), row),
                  pl.BlockSpec((tm, GDN_W), row),
                  pl.BlockSpec((tm, 2 * d), row),
                  pl.BlockSpec((n_mem, 2 * d), lambda i: (i // per_b, 0)),
                  pl.BlockSpec((SB_W, d), fixed),
                  pl.BlockSpec((GDN_W, d), fixed),
                  pl.BlockSpec((d, d), fixed),
                  pl.BlockSpec((1, d), fixed),
                  pl.BlockSpec((d, d), fixed),
                  pl.BlockSpec((d, d), fixed),
                  pl.BlockSpec((1, d), fixed)],
        out_specs=[pl.BlockSpec((tm, d), row), pl.BlockSpec((d, tm), lambda i: (0, i))],
        out_shape=[jax.ShapeDtypeStruct((t, d), F32), jax.ShapeDtypeStruct((d, t), BF16)],
        compiler_params=_params(("arbitrary",)),
        name="merge",
    )(x2d, osb, og, gates, kv, wsb, wg, wmix, g_cross, wcq, wco, g_peer)


N_HP = 2 * PEER_HEADS
N_TOP = PEER_TOPK + 1
PAIR_ROWS = [min(N_TOP, N_TOP // (a + 1)) for a in range(N_TOP)]
N_SINGLE = sum(1 for n in PAIR_ROWS if n == 1)
N_CAND = 64
assert sum(PAIR_ROWS) <= N_CAND
TOP_PAD = -(-N_TOP // 8) * 8


def _sort_network(n):
    pairs = []
    p = 1
    while p < n:
        k = p
        while k >= 1:
            for j in range(k % p, n - k, 2 * k):
                for i in range(min(k, n - j - k)):
                    if (i + j) // (2 * p) == (i + j + k) // (2 * p):
                        pairs.append((i + j, i + j + k))
            k //= 2
        p *= 2
    return pairs


def _peersel_kernel(xpt_ref, wpqt_ref, sk_ref, r2_ref, b2_ref, n1_ref, a1_ref,
                    qps, sall, tops, cand, best):
    qpt = _dot(wpqt_ref[...], xpt_ref[...])
    for hp in range(N_HP):
        qps[hp] = qpt[hp * PEER_KEYS:(hp + 1) * PEER_KEYS, :].astype(BF16)

    def top_values(s, dst_ref):
        sub = s.shape[0] // SUBLANES
        v = [s[k * SUBLANES:(k + 1) * SUBLANES, :] for k in range(sub)]
        for a, b in _sort_network(sub):
            v[a], v[b] = jnp.maximum(v[a], v[b]), jnp.minimum(v[a], v[b])
        for r in range(N_TOP):
            m = jnp.max(v[0], axis=0, keepdims=True)
            dst_ref[r:r + 1, :] = m
            hit = v[0] == m
            for k in range(min(sub, N_TOP - r - 1)):
                nxt = v[k + 1] if k + 1 < sub else jnp.full_like(v[k], -jnp.inf)
                v[k] = jnp.where(hit, nxt, v[k])

    def head_scores(h, carry):
        for p in range(2):
            s = _dot(sk_ref[2 * h + p], qps[2 * h + p])
            sall[2 * h + p] = s
            top_values(s, tops.at[2 * h + p])
        return carry

    lax.fori_loop(0, PEER_HEADS, head_scores, 0)

    n_multi = N_TOP - N_SINGLE
    cand[...] = jnp.full(cand.shape, -jnp.inf, F32)

    def head(h, carry):
        xs = tops[2 * h, 0:N_TOP, :]
        ys = tops[2 * h + 1, 0:N_TOP, :]
        row = 0
        for a in range(n_multi):
            cand[row:row + PAIR_ROWS[a], :] = xs[a:a + 1, :] + ys[0:PAIR_ROWS[a], :]
            row += PAIR_ROWS[a]
        cand[row:row + N_SINGLE, :] = xs[n_multi:N_TOP, :] + ys[0:1, :]
        top_values(cand[...], best)
        bs = best[0:N_TOP, :]
        top = bs[0:PEER_TOPK, :]
        zsum = jnp.sum(jnp.exp(top - top[0:1, :]), axis=0, keepdims=True)
        theta = 0.5 * (bs[PEER_TOPK - 1:PEER_TOPK, :] + bs[PEER_TOPK:PEER_TOPK + 1, :])
        s1 = sall[2 * h]
        s2 = sall[2 * h + 1]
        thr = theta - s1
        rank = jnp.zeros_like(s2)
        count = jnp.zeros_like(s1)
        for b in range(N_TOP):
            yb = ys[b:b + 1, :]
            rank = jnp.where(yb > s2, b + 1.0, rank)
            count = jnp.where(yb >= thr, b + 1.0, count)
        rows = pl.ds(pl.multiple_of(h * PEER_KEYS, PEER_KEYS), PEER_KEYS)
        rows16 = pl.ds(pl.multiple_of(h * (PEER_KEYS // 2), PEER_KEYS // 2), PEER_KEYS // 2)
        r2_ref[rows16, :] = pltpu.bitcast(rank.astype(BF16), jnp.int32)
        b2_ref[rows16, :] = pltpu.bitcast(jnp.exp(s2 - ys[0:1, :]).astype(BF16), jnp.int32)
        n1_ref[rows, :] = count
        a1_ref[rows, :] = jnp.exp(s1 - xs[0:1, :]) / zsum
        return carry

    lax.fori_loop(0, PEER_HEADS, head, 0)


def _peersel_call(xpt, wpqt, sk, tm):
    d, t = xpt.shape
    nq = wpqt.shape[0]
    kd = sk.shape[-1]
    out32 = jax.ShapeDtypeStruct((PEER_HEADS * PEER_KEYS, t), F32)
    out16 = jax.ShapeDtypeStruct((PEER_HEADS * PEER_KEYS // 2, t), jnp.int32)
    ospec = pl.BlockSpec((PEER_HEADS * PEER_KEYS, tm), lambda i: (0, i))
    ospec16 = pl.BlockSpec((PEER_HEADS * PEER_KEYS // 2, tm), lambda i: (0, i))
    return pl.pallas_call(
        _peersel_kernel,
        grid=(t // tm,),
        in_specs=[pl.BlockSpec((d, tm), lambda i: (0, i)),
                  pl.BlockSpec((nq, d), lambda i: (0, 0)),
                  pl.BlockSpec((N_HP, PEER_KEYS, kd), lambda i: (0, 0, 0))],
        out_specs=[ospec16, ospec16, ospec, ospec],
        out_shape=[out16, out16, out32, out32],
        scratch_shapes=[pltpu.VMEM((N_HP, kd, tm), BF16),
                        pltpu.VMEM((N_HP, PEER_KEYS, tm), F32),
                        pltpu.VMEM((N_HP, TOP_PAD, tm), F32),
                        pltpu.VMEM((N_CAND, tm), F32),
                        pltpu.VMEM((TOP_PAD, tm), F32)],
        compiler_params=_params(("arbitrary",)),
        name="peersel",
    )(xpt, wpqt, sk)


PEER_EBLK = 8 * PEER_KEYS
PEER_GRP = 2 * PEER_KEYS
PEER_RC = 16


def _gelu(x):
    return 0.5 * x * (1.0 + lax.erf(x * 0.7071067811865476))


def _peer_kernel(xpt_ref, u_ref, vt_ref, r2_ref, b2_ref, n1_ref, a1_ref, x2_ref, gf_ref, o_ref,
                 ht, at, acc):
    e = pl.program_id(1)

    @pl.when(e == 0)
    def _():
        acc[...] = jnp.zeros_like(acc)

    n_i = PEER_EBLK // PEER_KEYS
    n_grp = PEER_EBLK // PEER_GRP

    def scores(g):
        rows = slice(g * PEER_GRP, (g + 1) * PEER_GRP)
        ht[rows, :] = _dot(u_ref[rows, :], xpt_ref[...])

    def gates(sub):
        i = n_i * e + sub
        for lt in range(ht.shape[1] // LANES):
            tl = slice(lt * LANES, (lt + 1) * LANES)
            def row16(ref, h):
                row = ref[pl.ds(h * PEER_KEYS + i, 1), :][:, tl]
                return jnp.broadcast_to(row, (PEER_RC, LANES)).astype(BF16)
            cnt = [row16(n1_ref, h) for h in range(PEER_HEADS)]
            cf = [row16(a1_ref, h) for h in range(PEER_HEADS)]
            zero = jnp.zeros((PEER_RC, LANES), BF16)
            for r in range(PEER_KEYS // PEER_RC):
                j0 = r * PEER_RC
                w = None
                for h in range(PEER_HEADS):
                    jr = slice((h * PEER_KEYS + j0) // 2, (h * PEER_KEYS + j0 + PEER_RC) // 2)
                    r2 = pltpu.bitcast(r2_ref[jr, tl], BF16)
                    b2 = pltpu.bitcast(b2_ref[jr, tl], BF16)
                    wh = jnp.where(r2 < cnt[h], b2 * cf[h], zero)
                    w = wh if w is None else w + wh
                row0 = sub * PEER_KEYS + j0
                at[row0:row0 + PEER_RC, tl] = _gelu(ht[row0:row0 + PEER_RC, tl]).astype(BF16) * w

    scores(0)
    for g in range(n_grp):
        if g + 1 < n_grp:
            scores(g + 1)
        for sub in range(g * PEER_GRP // PEER_KEYS, (g + 1) * PEER_GRP // PEER_KEYS):
            gates(sub)
        if g % 2 == 1:
            rows = slice((g - 1) * PEER_GRP, (g + 1) * PEER_GRP)
            acc[...] += _dot(vt_ref[:, rows], at[rows, :])

    @pl.when(e == pl.num_programs(1) - 1)
    def _():
        x3 = x2_ref[...] + acc[...].T
        o_ref[...] = _rms(x3, gf_ref[...])


def _peer_call(xpt, u16, vt16, r2, b2, n1, a1, x2, g_final, tb):
    d, t = xpt.shape
    n_exp = u16.shape[0]
    tok = lambda i, e: (i, 0)
    sel = pl.BlockSpec((PEER_HEADS * PEER_KEYS, tb), lambda i, e: (0, i))
    sel16 = pl.BlockSpec((PEER_HEADS * PEER_KEYS // 2, tb), lambda i, e: (0, i))
    return pl.pallas_call(
        _peer_kernel,
        grid=(t // tb, n_exp // PEER_EBLK),
        in_specs=[pl.BlockSpec((d, tb), lambda i, e: (0, i)),
                  pl.BlockSpec((PEER_EBLK, d), lambda i, e: (e, 0)),
                  pl.BlockSpec((d, PEER_EBLK), lambda i, e: (0, e)),
                  sel16, sel16, sel, sel,
                  pl.BlockSpec((tb, d), tok),
                  pl.BlockSpec((1, d), lambda i, e: (0, 0))],
        out_specs=pl.BlockSpec((tb, d), tok),
        out_shape=jax.ShapeDtypeStruct((t, d), F32),
        scratch_shapes=[pltpu.VMEM((PEER_EBLK, tb), F32),
                        pltpu.VMEM((PEER_EBLK, tb), BF16),
                        pltpu.VMEM((d, tb), F32)],
        compiler_params=_params(("arbitrary", "arbitrary")),
        name="peer",
    )(xpt, u16, vt16, r2, b2, n1, a1, x2, g_final)


def _regroup_w_in(w_in):
    o_ba = 3 * SB_W + 4 * GDN_W
    ba = jnp.pad(w_in[:, o_ba:o_ba + 2 * GDN_HEADS], ((0, 0), (0, LANES - 2 * GDN_HEADS)))
    return jnp.concatenate([w_in[:, :o_ba], w_in[:, o_ba + 2 * GDN_HEADS:], ba], axis=1).astype(BF16)


def _layer(x2d, kv, batch, seq, n_mem, g_mix, w_in, gdn_conv, gdn_a_log, gdn_dt_bias, gdn_norm_g,
           w_sb_up, w_gdn_up, w_mix_out, g_cross, w_cq, w_co, g_peer, w_pq, peer_subkeys,
           peer_u, peer_v, g_out):
    t, d = x2d.shape
    tm = _pick(seq, (256, 128))
    sbqkv, gqkv, gz, gates, gba = _inproj_call(x2d, g_mix.reshape(1, d), _regroup_w_in(w_in), tm)
    o_sb = _sb_call(sbqkv, batch, seq)
    lane_pad = lambda v: jnp.pad(v.astype(F32), (GDN_HEADS, LANES - 2 * GDN_HEADS)).reshape(1, LANES)
    o_g = _gdn_call(gqkv, gz, gba, gdn_conv, lane_pad(gdn_a_log), lane_pad(gdn_dt_bias),
                    gdn_norm_g.reshape(1, GDN_HEAD_DIM), batch, seq, _pick(seq, (256, 128, 64)))
    x2, xpt = _merge_call(x2d, o_sb, o_g, gates, kv, w_sb_up.astype(BF16), w_gdn_up.astype(BF16),
                          w_mix_out.astype(BF16), g_cross.reshape(1, d), w_cq.astype(BF16),
                          w_co.astype(BF16), g_peer.reshape(1, d), seq, n_mem, tm)
    sk = peer_subkeys.reshape(N_HP, PEER_KEYS, -1).astype(BF16)
    r2, b2, n1, a1 = _peersel_call(xpt, w_pq.T.astype(BF16), sk, tm)
    return _peer_call(xpt, peer_u.astype(BF16), peer_v.T.astype(BF16), r2, b2, n1, a1, x2,
                      g_out.reshape(1, d), _pick(t, (512, 256, 128)))


def kernel(x, mem, g_mix, w_in, gdn_conv, gdn_a_log, gdn_dt_bias, gdn_norm_g, w_sb_up, w_gdn_up,
           w_mix_out, g_cross, g_mem, w_cq, w_ckv, w_co, g_peer, w_pq, peer_subkeys, peer_u, peer_v,
           g_final):
    batch, seq, d = x.shape
    n_mem = mem.shape[1]
    depth = g_mix.shape[0]
    assert depth == 1, "the final RMSNorm is fused into the last layer's PEER kernel"
    x2d = x.reshape(batch * seq, d)
    mem2d = mem.reshape(batch * n_mem, d)
    l = 0
    kv = _kv_call(mem2d, g_mem[l].reshape(1, d), w_ckv[l].astype(BF16), n_mem)
    out = _layer(x2d, kv, batch, seq, n_mem, g_mix[l], w_in[l], gdn_conv[l], gdn_a_log[l],
                 gdn_dt_bias[l], gdn_norm_g[l], w_sb_up[l], w_gdn_up[l], w_mix_out[l], g_cross[l],
                 w_cq[l], w_co[l], g_peer[l], w_pq[l], peer_subkeys[l], peer_u[l], peer_v[l], g_final)
    return out.reshape(batch, seq, d)
```

```python
import functools

import jax
import jax.numpy as jnp
from jax import lax
from jax.experimental import pallas as pl
from jax.experimental.pallas import tpu as pltpu

F32 = jnp.float32
BF16 = jnp.bfloat16

EPS = 1e-6
SB_HEADS = 8
SB_HEAD_DIM = 64
GDN_HEADS = 4
GDN_HEAD_DIM = 128
CONV_WIDTH = 4
CHUNK = 64
X_HEADS = 4
PEER_HEADS = 8
PEER_KEYS = 128
PEER_TOPK = 16

LANES = 128
SUBLANES = 8
VMEM_LIMIT = 56 * 1024 * 1024
SB_SKIP_LOG = -104.0


def _dot(a, b):
    return jnp.dot(a, b, preferred_element_type=F32)


def _dot_nt(a, b):
    return lax.dot_general(a, b, (((1,), (1,)), ((), ())), preferred_element_type=F32)


def _dot_tn(a, b):
    return lax.dot_general(a, b, (((0,), (0,)), ((), ())), preferred_element_type=F32)


def _split3(a):
    hi = a.astype(BF16)
    r1 = a - hi.astype(F32)
    mid = r1.astype(BF16)
    lo = (r1 - mid.astype(F32)).astype(BF16)
    return hi, mid, lo


def _dot_exact_lhs(m_bf16, b):
    hi, mid, lo = _split3(b)
    return _dot(m_bf16, hi) + _dot(m_bf16, mid) + _dot(m_bf16, lo)


def _split2(a):
    hi = a.astype(BF16)
    return hi, (a - hi.astype(F32)).astype(BF16)


def _dot_hp2(a2, b2):
    return _dot(a2[0], b2[0]) + _dot(a2[0], b2[1]) + _dot(a2[1], b2[0])


def _rms(x, g):
    return x * lax.rsqrt(jnp.mean(x * x, axis=-1, keepdims=True) + EPS) * g


def _softplus(x):
    return jnp.maximum(x, 0.0) + jnp.log1p(jnp.exp(-jnp.abs(x)))


def _sigmoid(x):
    return 1.0 / (1.0 + jnp.exp(-x))


def _params(sem):
    return pltpu.CompilerParams(dimension_semantics=sem, vmem_limit_bytes=VMEM_LIMIT)


def _pick(n, pref):
    for c in pref:
        if n % c == 0:
            return c
    return n


def _kv_kernel(mem_ref, g_ref, w_ref, o_ref):
    mn = _rms(mem_ref[...], g_ref[...]).astype(BF16)
    o_ref[...] = _dot(mn, w_ref[...]).astype(BF16)


def _kv_call(mem2d, g_mem, w_ckv, n_mem):
    rows, d = mem2d.shape
    return pl.pallas_call(
        _kv_kernel,
        grid=(rows // n_mem,),
        in_specs=[pl.BlockSpec((n_mem, d), lambda i: (i, 0)),
                  pl.BlockSpec((1, d), lambda i: (0, 0)),
                  pl.BlockSpec((d, 2 * d), lambda i: (0, 0))],
        out_specs=pl.BlockSpec((n_mem, 2 * d), lambda i: (i, 0)),
        out_shape=jax.ShapeDtypeStruct((rows, 2 * d), BF16),
        compiler_params=_params(("arbitrary",)),
        name="kv",
    )(mem2d, g_mem, w_ckv)


SB_W = SB_HEADS * SB_HEAD_DIM
GDN_W = GDN_HEADS * GDN_HEAD_DIM
C_SB = (0, 3 * SB_W)
C_GQ = (C_SB[1], C_SB[1] + 3 * GDN_W)
C_GZ = (C_GQ[1], C_GQ[1] + GDN_W)


def _inproj_kernel(x_ref, g_ref, w_ref, sb_ref, gq_ref, gz_ref, gt_ref, ba_ref, *, d):
    xn = _rms(x_ref[...], g_ref[...]).astype(BF16)
    c_gt = (C_GZ[1], C_GZ[1] + 2 * d)
    c_ba = (c_gt[1], c_gt[1] + LANES)

    def mm(c):
        return _dot(xn, w_ref[:, c[0]:c[1]])

    sb_ref[...] = mm(C_SB).astype(BF16)
    gq_ref[...] = mm(C_GQ)
    gz_ref[...] = mm(C_GZ)
    gt_ref[...] = mm(c_gt).astype(BF16)
    ba_ref[...] = mm(c_ba)


def _inproj_call(x2d, g_mix, w_p, tm):
    t, d = x2d.shape
    n = w_p.shape[1]
    row = lambda i: (i, 0)
    fixed = lambda i: (0, 0)
    return pl.pallas_call(
        functools.partial(_inproj_kernel, d=d),
        grid=(t // tm,),
        in_specs=[pl.BlockSpec((tm, d), row),
                  pl.BlockSpec((1, d), fixed),
                  pl.BlockSpec((d, n), fixed)],
        out_specs=[pl.BlockSpec((tm, 3 * SB_W), row),
                   pl.BlockSpec((tm, 3 * GDN_W), row),
                   pl.BlockSpec((tm, GDN_W), row),
                   pl.BlockSpec((tm, 2 * d), row),
                   pl.BlockSpec((tm, LANES), row)],
        out_shape=[jax.ShapeDtypeStruct((t, 3 * SB_W), BF16),
                   jax.ShapeDtypeStruct((t, 3 * GDN_W), F32),
                   jax.ShapeDtypeStruct((t, GDN_W), F32),
                   jax.ShapeDtypeStruct((t, 2 * d), BF16),
                   jax.ShapeDtypeStruct((t, LANES), F32)],
        compiler_params=_params(("arbitrary",)),
        name="inproj",
    )(x2d, g_mix, w_p)


SB_GROUP = 2 * LANES
SB_HPG = SB_GROUP // SB_HEAD_DIM


def _sb_kernel(q_ref, k_ref, v_ref, o_ref, acc_ref, c_ref, *, seq, blk):
    scale = SB_HEAD_DIM ** -0.5
    lane = lax.broadcasted_iota(jnp.int32, (blk, SB_GROUP), 1)
    in_head = [jnp.logical_and(lane >= h * SB_HEAD_DIM, lane < (h + 1) * SB_HEAD_DIM)
               for h in range(SB_HPG)]
    r_i = lax.broadcasted_iota(jnp.int32, (blk, blk), 0)
    c_i = lax.broadcasted_iota(jnp.int32, (blk, blk), 1)
    causal = c_i < r_i
    jj = lax.broadcasted_iota(jnp.int32, (blk, 2 * blk), 0)
    ss = lax.broadcasted_iota(jnp.int32, (blk, 2 * blk), 1)
    m_ext = jnp.where(ss >= blk, 1.0, jnp.where(jj > ss, 1.0, 0.0)).astype(BF16)

    def all_heads(q, kblk, vblk, masked):
        zero = jnp.zeros_like(q)
        hs = range(SB_HPG)
        z = [_dot_nt(jnp.where(in_head[h], q, zero), kblk) * scale for h in hs]
        sp = [_softplus(z[h]) for h in hs]
        lk = [-sp[h] for h in hs]
        if masked:
            lk = [jnp.where(causal, lk[h], 0.0) for h in hs]
        parts = [_split2(lk[h]) for h in hs]
        ext = [_dot(parts[h][0], m_ext) + _dot(parts[h][1], m_ext) for h in hs]
        cm = None
        for h in hs:
            c = c_ref[h]
            a = jnp.exp((z[h] - sp[h]) + ext[h][:, :blk] + c)
            if masked:
                a = jnp.where(causal, a, 0.0)
            acc_ref[h] += _dot(a.astype(BF16), vblk)
            c_new = c + ext[h][:, blk:]
            c_ref[h] = c_new
            cm = c_new if cm is None else jnp.maximum(cm, c_new)
        return jnp.max(cm)

    def qblock(qi, carry):
        q0 = pl.multiple_of(qi * blk, blk)
        q = q_ref[pl.ds(q0, blk), :]
        acc_ref[...] = jnp.zeros_like(acc_ref)
        c_ref[...] = jnp.zeros_like(c_ref)
        m0 = all_heads(q, k_ref[pl.ds(q0, blk), :], v_ref[pl.ds(q0, blk), :], True)

        def cond(st):
            kb, cm = st
            return jnp.logical_and(kb >= 0, cm > SB_SKIP_LOG)

        def body(st):
            kb, _ = st
            k0 = pl.multiple_of(kb * blk, blk)
            return kb - 1, all_heads(q, k_ref[pl.ds(k0, blk), :], v_ref[pl.ds(k0, blk), :], False)

        lax.while_loop(cond, body, (qi - 1, m0))
        o = acc_ref[0]
        for h in range(1, SB_HPG):
            o = jnp.where(in_head[h], acc_ref[h], o)
        o_ref[pl.ds(q0, blk), :] = o.astype(BF16)
        return carry

    lax.fori_loop(0, seq // blk, qblock, 0)


def _sb_call(sbqkv, batch, seq):
    t = sbqkv.shape[0]
    ng = SB_W // SB_GROUP
    blk = _pick(seq, (256, 128))
    return pl.pallas_call(
        functools.partial(_sb_kernel, seq=seq, blk=blk),
        grid=(batch, ng),
        in_specs=[pl.BlockSpec((seq, SB_GROUP), lambda b, g: (b, g)),
                  pl.BlockSpec((seq, SB_GROUP), lambda b, g: (b, ng + g)),
                  pl.BlockSpec((seq, SB_GROUP), lambda b, g: (b, 2 * ng + g))],
        out_specs=pl.BlockSpec((seq, SB_GROUP), lambda b, g: (b, g)),
        out_shape=jax.ShapeDtypeStruct((t, SB_W), BF16),
        scratch_shapes=[pltpu.VMEM((SB_HPG, blk, SB_GROUP), F32),
                        pltpu.VMEM((SB_HPG, blk, blk), F32)],
        compiler_params=_params(("arbitrary", "arbitrary")),
        name="sb",
    )(sbqkv, sbqkv, sbqkv)


GDN_PAD = 8
GDN_BASE = 8


def _gdn_kernel(qkv_ref, z_ref, ba_ref, conv_ref, alog_ref, dtb_ref, ng_ref, o_ref,
                xbuf, u_s, w_s, qd_s, kd_s, qk_s, vn_s, os_, state, *, blk):
    hd = GDN_HEAD_DIM
    w3 = 3 * GDN_W
    nh = GDN_HEADS

    @pl.when(pl.program_id(1) == 0)
    def _():
        state[...] = jnp.zeros_like(state)
        xbuf[0:GDN_PAD, :] = jnp.zeros((GDN_PAD, w3), F32)

    xbuf[GDN_PAD:GDN_PAD + blk, :] = qkv_ref[...]
    cw = conv_ref[...]
    y = xbuf[GDN_PAD:GDN_PAD + blk, :] * cw[CONV_WIDTH - 1:CONV_WIDTH, :]
    for j in range(CONV_WIDTH - 1):
        d = CONV_WIDTH - 1 - j
        y = y + xbuf[GDN_PAD - d:GDN_PAD - d + blk, :] * cw[j:j + 1, :]
    xbuf[0:GDN_PAD, :] = xbuf[blk:blk + GDN_PAD, :]
    y = y * _sigmoid(y)

    ba = ba_ref[...]
    lane = lax.broadcasted_iota(jnp.int32, (blk, LANES), 1)
    g = -jnp.exp(alog_ref[...]) * _softplus(ba + dtb_ref[...])
    g = jnp.where(jnp.logical_and(lane >= nh, lane < 2 * nh), g, 0.0)
    rr = lax.broadcasted_iota(jnp.int32, (blk, blk), 0)
    cc = lax.broadcasted_iota(jnp.int32, (blk, blk), 1)
    same = (rr // CHUNK) == (cc // CHUNK)
    eye = jnp.where(rr == cc, 1.0, 0.0)
    bd_ones = jnp.where(same, 1.0, 0.0)
    bd_incl = jnp.where(cc <= rr, bd_ones, 0.0)
    bd_strict = bd_incl - eye
    gc = _dot_exact_lhs(bd_incl.astype(BF16), g)
    gl = _dot_exact_lhs(bd_ones.astype(BF16), g)
    eg = jnp.exp(gc)
    ek = jnp.exp(gl - gc)
    ec = jnp.exp(gl)
    beta = _sigmoid(ba)
    incl = bd_incl > 0.5

    heads = range(nh)
    a_s, rhs = [], []
    for h in heads:
        qh = y[:, h * hd:(h + 1) * hd]
        kh = y[:, GDN_W + h * hd:GDN_W + (h + 1) * hd]
        vh = y[:, 2 * GDN_W + h * hd:2 * GDN_W + (h + 1) * hd]
        q = qh * (lax.rsqrt(jnp.sum(qh * qh, axis=-1, keepdims=True) + EPS) * (hd ** -0.5))
        k = kh * lax.rsqrt(jnp.sum(kh * kh, axis=-1, keepdims=True) + EPS)
        gcol = gc[:, nh + h:nh + h + 1]
        grow = jnp.sum(eye * gcol, axis=0, keepdims=True)
        dm = jnp.exp(jnp.where(incl, gcol - grow, -jnp.inf))
        bcol = beta[:, h:h + 1]
        kb = k * bcol
        k16 = k.astype(BF16)
        a = _dot_nt(kb.astype(BF16), k16) * dm * bd_strict
        egc = eg[:, nh + h:nh + h + 1]
        qk_s[h] = (_dot_nt(q.astype(BF16), k16) * dm).astype(BF16)
        qd_s[h] = (q * egc).astype(BF16)
        kd_s[h] = (k * ek[:, nh + h:nh + h + 1]).astype(BF16)
        rhs.append(_split2(jnp.concatenate([vh * bcol, kb * egc], axis=1)))
        a_s.append(a)
    def mm(p, q):
        return _dot(p.astype(BF16), q.astype(BF16))

    in_base = jnp.where(rr // GDN_BASE == cc // GDN_BASE, 1.0, 0.0)
    xs = []
    for h in heads:
        n = -(a_s[h] * in_base)
        x = eye + n
        p = mm(n, n)
        w = 2
        while w < GDN_BASE:
            x = x + mm(x, p)
            w *= 2
            if w < GDN_BASE:
                p = mm(p, p)
        xs.append(x)
    w = GDN_BASE
    while w < CHUNK:
        m_off = jnp.where(jnp.logical_and(rr // w != cc // w, rr // (2 * w) == cc // (2 * w)), 1.0, 0.0)
        xs = [xs[h] - mm(mm(xs[h], a_s[h] * m_off), xs[h]) for h in heads]
        w *= 2
    for h in heads:
        sol = _dot_hp2(_split2(xs[h]), rhs[h])
        u_s[h] = sol[:, :hd]
        w_s[h] = sol[:, hd:].astype(BF16)

    vn_s[...] = jnp.zeros_like(vn_s)
    for c in range(blk // CHUNK):
        rows = slice(c * CHUNK, (c + 1) * CHUNK)
        for h in range(nh):
            s = state[h]
            s16 = s.astype(BF16)
            vnew = u_s[h, rows, :] - _dot(w_s[h, rows, :], s16)
            vn16 = vnew.astype(BF16)
            vn_s[h, rows, :] = vn16
            os_[rows, h * hd:(h + 1) * hd] = (_dot(qd_s[h, rows, :], s16)
                                              + _dot(qk_s[h, rows, :], vn_s[h]))
            cd = ec[c * CHUNK:c * CHUNK + 1, nh + h:nh + h + 1]
            state[h] = s * cd + _dot_tn(kd_s[h, rows, :], vn16)

    zg = z_ref[...]
    ng = ng_ref[...]
    for h in range(nh):
        cols = slice(h * hd, (h + 1) * hd)
        oh = _rms(os_[:, cols], ng)
        zh = zg[:, cols]
        o_ref[:, cols] = (oh * (zh * _sigmoid(zh))).astype(BF16)


def _gdn_call(gqkv, gz, gba, conv_w, alog_l, dtb_l, norm_g, batch, seq, blk):
    t = gqkv.shape[0]
    nb = seq // blk
    row = lambda b, i: (b * nb + i, 0)
    fixed = lambda b, i: (0, 0)
    w3 = 3 * GDN_W
    nh, hd = GDN_HEADS, GDN_HEAD_DIM
    return pl.pallas_call(
        functools.partial(_gdn_kernel, blk=blk),
        grid=(batch, nb),
        in_specs=[pl.BlockSpec((blk, w3), row),
                  pl.BlockSpec((blk, GDN_W), row),
                  pl.BlockSpec((blk, LANES), row),
                  pl.BlockSpec((CONV_WIDTH, w3), fixed),
                  pl.BlockSpec((1, LANES), fixed),
                  pl.BlockSpec((1, LANES), fixed),
                  pl.BlockSpec((1, hd), fixed)],
        out_specs=pl.BlockSpec((blk, GDN_W), row),
        out_shape=jax.ShapeDtypeStruct((t, GDN_W), BF16),
        scratch_shapes=[pltpu.VMEM((blk + GDN_PAD, w3), F32),
                        pltpu.VMEM((nh, blk, hd), F32),
                        pltpu.VMEM((nh, blk, hd), BF16),
                        pltpu.VMEM((nh, blk, hd), BF16),
                        pltpu.VMEM((nh, blk, hd), BF16),
                        pltpu.VMEM((nh, blk, blk), BF16),
                        pltpu.VMEM((nh, blk, hd), BF16),
                        pltpu.VMEM((blk, GDN_W), F32),
                        pltpu.VMEM((nh, hd, hd), F32)],
        compiler_params=_params(("arbitrary", "arbitrary")),
        name="gdn",
    )(gqkv, gz, gba, conv_w, alog_l, dtb_l, norm_g)


def _merge_kernel(x_ref, osb_ref, og_ref, gt_ref, kv_ref, wsb_ref, wg_ref, wmix_ref,
                  gcr_ref, wcq_ref, wco_ref, gpe_ref, x2_ref, xpt_ref, *, d):
    gates = gt_ref[...].astype(F32)
    merged = (_sigmoid(gates[:, :d]) * _dot(osb_ref[...], wsb_ref[...])
              + _sigmoid(gates[:, d:]) * _dot(og_ref[...], wg_ref[...]))
    x1 = x_ref[...] + _dot(merged.astype(BF16), wmix_ref[...])
    hn = _rms(x1, gcr_ref[...]).astype(BF16)
    q = _dot(hn, wcq_ref[...])
    xd = d // X_HEADS
    kv = kv_ref[...]
    hs = range(X_HEADS)
    sc = [_dot_nt(q[:, h * xd:(h + 1) * xd].astype(BF16), kv[:, h * xd:(h + 1) * xd]) * (xd ** -0.5)
          for h in hs]
    ex = [jnp.exp(sc[h] - jnp.max(sc[h], axis=-1, keepdims=True)) for h in hs]
    pr = [(ex[h] / jnp.sum(ex[h], axis=-1, keepdims=True)).astype(BF16) for h in hs]
    outs = [_dot(pr[h], kv[:, d + h * xd:d + (h + 1) * xd]) for h in hs]
    o = jnp.concatenate(outs, axis=-1).astype(BF16)
    x2 = x1 + _dot(o, wco_ref[...])
    x2_ref[...] = x2
    xpt_ref[...] = _rms(x2, gpe_ref[...]).T.astype(BF16)


def _merge_call(x2d, osb, og, gates, kv, wsb, wg, wmix, g_cross, wcq, wco, g_peer, seq, n_mem, tm):
    t, d = x2d.shape
    per_b = seq // tm
    row = lambda i: (i, 0)
    fixed = lambda i: (0, 0)
    return pl.pallas_call(
        functools.partial(_merge_kernel, d=d),
        grid=(t // tm,),
        in_specs=[pl.BlockSpec((tm, d), row),
                  pl.BlockSpec((tm, SB_W), row),
                  pl.BlockSpec((tm, GDN_W), row),
                  pl.BlockSpec((tm, 2 * d), row),
                  pl.BlockSpec((n_mem, 2 * d), lambda i: (i // per_b, 0)),
                  pl.BlockSpec((SB_W, d), fixed),
                  pl.BlockSpec((GDN_W, d), fixed),
                  pl.BlockSpec((d, d), fixed),
                  pl.BlockSpec((1, d), fixed),
                  pl.BlockSpec((d, d), fixed),
                  pl.BlockSpec((d, d), fixed),
                  pl.BlockSpec((1, d), fixed)],
        out_specs=[pl.BlockSpec((tm, d), row), pl.BlockSpec((d, tm), lambda i: (0, i))],
        out_shape=[jax.ShapeDtypeStruct((t, d), F32), jax.ShapeDtypeStruct((d, t), BF16)],
        compiler_params=_params(("arbitrary",)),
        name="merge",
    )(x2d, osb, og, gates, kv, wsb, wg, wmix, g_cross, wcq, wco, g_peer)


N_HP = 2 * PEER_HEADS
N_TOP = PEER_TOPK + 1
PAIR_ROWS = [min(N_TOP, N_TOP // (a + 1)) for a in range(N_TOP)]
N_SINGLE = sum(1 for n in PAIR_ROWS if n == 1)
N_CAND = 64
assert sum(PAIR_ROWS) <= N_CAND
TOP_PAD = -(-N_TOP // 8) * 8


def _sort_network(n):
    pairs = []
    p = 1
    while p < n:
        k = p
        while k >= 1:
            for j in range(k % p, n - k, 2 * k):
                for i in range(min(k, n - j - k)):
                    if (i + j) // (2 * p) == (i + j + k) // (2 * p):
                        pairs.append((i + j, i + j + k))
            k //= 2
        p *= 2
    return pairs


def _peersel_kernel(xpt_ref, wpqt_ref, sk_ref, r2_ref, b2_ref, n1_ref, a1_ref,
                    qps, sall, tops, cand, best):
    qpt = _dot(wpqt_ref[...], xpt_ref[...])
    for hp in range(N_HP):
        qps[hp] = qpt[hp * PEER_KEYS:(hp + 1) * PEER_KEYS, :].astype(BF16)

    def top_values(s, dst_ref):
        sub = s.shape[0] // SUBLANES
        v = [s[k * SUBLANES:(k + 1) * SUBLANES, :] for k in range(sub)]
        for a, b in _sort_network(sub):
            v[a], v[b] = jnp.maximum(v[a], v[b]), jnp.minimum(v[a], v[b])
        for r in range(N_TOP):
            m = jnp.max(v[0], axis=0, keepdims=True)
            dst_ref[r:r + 1, :] = m
            hit = v[0] == m
            for k in range(min(sub, N_TOP - r - 1)):
                nxt = v[k + 1] if k + 1 < sub else jnp.full_like(v[k], -jnp.inf)
                v[k] = jnp.where(hit, nxt, v[k])

    def head_scores(h, carry):
        for p in range(2):
            s = _dot(sk_ref[2 * h + p], qps[2 * h + p])
            sall[2 * h + p] = s
            top_values(s, tops.at[2 * h + p])
        return carry

    lax.fori_loop(0, PEER_HEADS, head_scores, 0)

    n_multi = N_TOP - N_SINGLE
    cand[...] = jnp.full(cand.shape, -jnp.inf, F32)

    def head(h, carry):
        xs = tops[2 * h, 0:N_TOP, :]
        ys = tops[2 * h + 1, 0:N_TOP, :]
        row = 0
        for a in range(n_multi):
            cand[row:row + PAIR_ROWS[a], :] = xs[a:a + 1, :] + ys[0:PAIR_ROWS[a], :]
            row += PAIR_ROWS[a]
        cand[row:row + N_SINGLE, :] = xs[n_multi:N_TOP, :] + ys[0:1, :]
        top_values(cand[...], best)
        bs = best[0:N_TOP, :]
        top = bs[0:PEER_TOPK, :]
        zsum = jnp.sum(jnp.exp(top - top[0:1, :]), axis=0, keepdims=True)
        theta = 0.5 * (bs[PEER_TOPK - 1:PEER_TOPK, :] + bs[PEER_TOPK:PEER_TOPK + 1, :])
        s1 = sall[2 * h]
        s2 = sall[2 * h + 1]
        thr = theta - s1
        rank = jnp.zeros_like(s2)
        count = jnp.zeros_like(s1)
        for b in range(N_TOP):
            yb = ys[b:b + 1, :]
            rank = jnp.where(yb > s2, b + 1.0, rank)
            count = jnp.where(yb >= thr, b + 1.0, count)
        rows = pl.ds(pl.multiple_of(h * PEER_KEYS, PEER_KEYS), PEER_KEYS)
        rows16 = pl.ds(pl.multiple_of(h * (PEER_KEYS // 2), PEER_KEYS // 2), PEER_KEYS // 2)
        r2_ref[rows16, :] = pltpu.bitcast(rank.astype(BF16), jnp.int32)
        b2_ref[rows16, :] = pltpu.bitcast(jnp.exp(s2 - ys[0:1, :]).astype(BF16), jnp.int32)
        n1_ref[rows, :] = count
        a1_ref[rows, :] = jnp.exp(s1 - xs[0:1, :]) / zsum
        return carry

    lax.fori_loop(0, PEER_HEADS, head, 0)


def _peersel_call(xpt, wpqt, sk, tm, tb):
    d, t = xpt.shape
    nq = wpqt.shape[0]
    kd = sk.shape[-1]
    per = tb // tm
    rows = PEER_HEADS * PEER_KEYS
    out32 = jax.ShapeDtypeStruct((t // tb, rows, tb), F32)
    out16 = jax.ShapeDtypeStruct((t // tb, rows // 2, tb), jnp.int32)
    ospec = pl.BlockSpec((None, rows, tm), lambda i: (i // per, 0, i % per))
    ospec16 = pl.BlockSpec((None, rows // 2, tm), lambda i: (i // per, 0, i % per))
    return pl.pallas_call(
        _peersel_kernel,
        grid=(t // tm,),
        in_specs=[pl.BlockSpec((d, tm), lambda i: (0, i)),
                  pl.BlockSpec((nq, d), lambda i: (0, 0)),
                  pl.BlockSpec((N_HP, PEER_KEYS, kd), lambda i: (0, 0, 0))],
        out_specs=[ospec16, ospec16, ospec, ospec],
        out_shape=[out16, out16, out32, out32],
        scratch_shapes=[pltpu.VMEM((N_HP, kd, tm), BF16),
                        pltpu.VMEM((N_HP, PEER_KEYS, tm), F32),
                        pltpu.VMEM((N_HP, TOP_PAD, tm), F32),
                        pltpu.VMEM((N_CAND, tm), F32),
                        pltpu.VMEM((TOP_PAD, tm), F32)],
        compiler_params=_params(("arbitrary",)),
        name="peersel",
    )(xpt, wpqt, sk)


PEER_EBLK = 8 * PEER_KEYS
PEER_GRP = 2 * PEER_KEYS
PEER_RC = 16


def _gelu(x):
    return 0.5 * x * (1.0 + lax.erf(x * 0.7071067811865476))


def _peer_kernel(xpt_ref, u_ref, vt_ref, r2_ref, b2_ref, n1_ref, a1_ref, x2_ref, gf_ref, o_ref,
                 ht, at, acc):
    e = pl.program_id(1)

    @pl.when(e == 0)
    def _():
        acc[...] = jnp.zeros_like(acc)

    n_i = PEER_EBLK // PEER_KEYS
    n_grp = PEER_EBLK // PEER_GRP

    def scores(g):
        rows = slice(g * PEER_GRP, (g + 1) * PEER_GRP)
        ht[rows, :] = _dot(u_ref[rows, :], xpt_ref[...])

    def gates(sub):
        i = n_i * e + sub
        for lt in range(ht.shape[1] // LANES):
            tl = slice(lt * LANES, (lt + 1) * LANES)
            def row16(ref, h):
                row = ref[pl.ds(h * PEER_KEYS + i, 1), :][:, tl]
                return jnp.broadcast_to(row, (PEER_RC, LANES)).astype(BF16)
            cnt = [row16(n1_ref, h) for h in range(PEER_HEADS)]
            cf = [row16(a1_ref, h) for h in range(PEER_HEADS)]
            zero = jnp.zeros((PEER_RC, LANES), BF16)
            for r in range(PEER_KEYS // PEER_RC):
                j0 = r * PEER_RC
                w = None
                for h in range(PEER_HEADS):
                    jr = slice((h * PEER_KEYS + j0) // 2, (h * PEER_KEYS + j0 + PEER_RC) // 2)
                    r2 = pltpu.bitcast(r2_ref[jr, tl], BF16)
                    b2 = pltpu.bitcast(b2_ref[jr, tl], BF16)
                    wh = jnp.where(r2 < cnt[h], b2 * cf[h], zero)
                    w = wh if w is None else w + wh
                row0 = sub * PEER_KEYS + j0
                at[row0:row0 + PEER_RC, tl] = _gelu(ht[row0:row0 + PEER_RC, tl]).astype(BF16) * w

    scores(0)
    for g in range(n_grp):
        if g + 1 < n_grp:
            scores(g + 1)
        for sub in range(g * PEER_GRP // PEER_KEYS, (g + 1) * PEER_GRP // PEER_KEYS):
            gates(sub)
        if g % 2 == 1:
            rows = slice((g - 1) * PEER_GRP, (g + 1) * PEER_GRP)
            acc[...] += _dot(vt_ref[:, rows], at[rows, :])

    @pl.when(e == pl.num_programs(1) - 1)
    def _():
        x3 = x2_ref[...] + acc[...].T
        o_ref[...] = _rms(x3, gf_ref[...])


def _peer_call(xpt, u16, vt16, r2, b2, n1, a1, x2, g_final, tb):
    d, t = xpt.shape
    n_exp = u16.shape[0]
    tok = lambda i, e: (i, 0)
    sel = pl.BlockSpec((None, PEER_HEADS * PEER_KEYS, tb), lambda i, e: (i, 0, 0))
    sel16 = pl.BlockSpec((None, PEER_HEADS * PEER_KEYS // 2, tb), lambda i, e: (i, 0, 0))
    return pl.pallas_call(
        _peer_kernel,
        grid=(t // tb, n_exp // PEER_EBLK),
        in_specs=[pl.BlockSpec((d, tb), lambda i, e: (0, i)),
                  pl.BlockSpec((PEER_EBLK, d), lambda i, e: (e, 0)),
                  pl.BlockSpec((None, d, PEER_EBLK), lambda i, e: (e, 0, 0)),
                  sel16, sel16, sel, sel,
                  pl.BlockSpec((tb, d), tok),
                  pl.BlockSpec((1, d), lambda i, e: (0, 0))],
        out_specs=pl.BlockSpec((tb, d), tok),
        out_shape=jax.ShapeDtypeStruct((t, d), F32),
        scratch_shapes=[pltpu.VMEM((PEER_EBLK, tb), F32),
                        pltpu.VMEM((PEER_EBLK, tb), BF16),
                        pltpu.VMEM((d, tb), F32)],
        compiler_params=_params(("arbitrary", "arbitrary")),
        name="peer",
    )(xpt, u16, vt16, r2, b2, n1, a1, x2, g_final)


def _regroup_w_in(w_in):
    o_ba = 3 * SB_W + 4 * GDN_W
    ba = jnp.pad(w_in[:, o_ba:o_ba + 2 * GDN_HEADS], ((0, 0), (0, LANES - 2 * GDN_HEADS)))
    return jnp.concatenate([w_in[:, :o_ba], w_in[:, o_ba + 2 * GDN_HEADS:], ba], axis=1).astype(BF16)


def _layer(x2d, kv, batch, seq, n_mem, g_mix, w_in, gdn_conv, gdn_a_log, gdn_dt_bias, gdn_norm_g,
           w_sb_up, w_gdn_up, w_mix_out, g_cross, w_cq, w_co, g_peer, w_pq, peer_subkeys,
           peer_u, peer_v, g_out):
    t, d = x2d.shape
    tm = _pick(seq, (256, 128))
    sbqkv, gqkv, gz, gates, gba = _inproj_call(x2d, g_mix.reshape(1, d), _regroup_w_in(w_in), tm)
    o_sb = _sb_call(sbqkv, batch, seq)
    lane_pad = lambda v: jnp.pad(v.astype(F32), (GDN_HEADS, LANES - 2 * GDN_HEADS)).reshape(1, LANES)
    o_g = _gdn_call(gqkv, gz, gba, gdn_conv, lane_pad(gdn_a_log), lane_pad(gdn_dt_bias),
                    gdn_norm_g.reshape(1, GDN_HEAD_DIM), batch, seq, _pick(seq, (256, 128, 64)))
    x2, xpt = _merge_call(x2d, o_sb, o_g, gates, kv, w_sb_up.astype(BF16), w_gdn_up.astype(BF16),
                          w_mix_out.astype(BF16), g_cross.reshape(1, d), w_cq.astype(BF16),
                          w_co.astype(BF16), g_peer.reshape(1, d), seq, n_mem, tm)
    sk = peer_subkeys.reshape(N_HP, PEER_KEYS, -1).astype(BF16)
    tb = _pick(t, (512, 256, 128))
    r2, b2, n1, a1 = _peersel_call(xpt, w_pq.T.astype(BF16), sk, min(tm, tb), tb)
    vt = peer_v.astype(BF16).reshape(-1, PEER_EBLK, d).transpose(0, 2, 1)
    return _peer_call(xpt, peer_u.astype(BF16), vt, r2, b2, n1, a1, x2, g_out.reshape(1, d), tb)


def kernel(x, mem, g_mix, w_in, gdn_conv, gdn_a_log, gdn_dt_bias, gdn_norm_g, w_sb_up, w_gdn_up,
           w_mix_out, g_cross, g_mem, w_cq, w_ckv, w_co, g_peer, w_pq, peer_subkeys, peer_u, peer_v,
           g_final):
    batch, seq, d = x.shape
    n_mem = mem.shape[1]
    depth = g_mix.shape[0]
    assert depth == 1, "the final RMSNorm is fused into the last layer's PEER kernel"
    x2d = x.reshape(batch * seq, d)
    mem2d = mem.reshape(batch * n_mem, d)
    l = 0
    kv = _kv_call(mem2d, g_mem[l].reshape(1, d), w_ckv[l].astype(BF16), n_mem)
    out = _layer(x2d, kv, batch, seq, n_mem, g_mix[l], w_in[l], gdn_conv[l], gdn_a_log[l],
                 gdn_dt_bias[l], gdn_norm_g[l], w_sb_up[l], w_gdn_up[l], w_mix_out[l], g_cross[l],
                 w_cq[l], w_co[l], g_peer[l], w_pq[l], peer_subkeys[l], peer_u[l], peer_v[l], g_final)
    return out.reshape(batch, seq, d)
```

```python
import functools

import jax
import jax.numpy as jnp
from jax import lax
from jax.experimental import pallas as pl
from jax.experimental.pallas import tpu as pltpu

F32 = jnp.float32
BF16 = jnp.bfloat16

EPS = 1e-6
SB_HEADS = 8
SB_HEAD_DIM = 64
GDN_HEADS = 4
GDN_HEAD_DIM = 128
CONV_WIDTH = 4
CHUNK = 64
X_HEADS = 4
PEER_HEADS = 8
PEER_KEYS = 128
PEER_TOPK = 16

LANES = 128
SUBLANES = 8
VMEM_LIMIT = 56 * 1024 * 1024
SB_SKIP_LOG = -104.0


def _dot(a, b):
    return jnp.dot(a, b, preferred_element_type=F32)


def _dot_nt(a, b):
    return lax.dot_general(a, b, (((1,), (1,)), ((), ())), preferred_element_type=F32)


def _dot_tn(a, b):
    return lax.dot_general(a, b, (((0,), (0,)), ((), ())), preferred_element_type=F32)


def _split3(a):
    hi = a.astype(BF16)
    r1 = a - hi.astype(F32)
    mid = r1.astype(BF16)
    lo = (r1 - mid.astype(F32)).astype(BF16)
    return hi, mid, lo


def _dot_exact_lhs(m_bf16, b):
    hi, mid, lo = _split3(b)
    return _dot(m_bf16, hi) + _dot(m_bf16, mid) + _dot(m_bf16, lo)


def _split2(a):
    hi = a.astype(BF16)
    return hi, (a - hi.astype(F32)).astype(BF16)


def _dot_hp2(a2, b2):
    return _dot(a2[0], b2[0]) + _dot(a2[0], b2[1]) + _dot(a2[1], b2[0])


def _rms(x, g):
    return x * lax.rsqrt(jnp.mean(x * x, axis=-1, keepdims=True) + EPS) * g


def _softplus(x):
    return jnp.maximum(x, 0.0) + jnp.log1p(jnp.exp(-jnp.abs(x)))


def _sigmoid(x):
    return 1.0 / (1.0 + jnp.exp(-x))


def _params(sem):
    return pltpu.CompilerParams(dimension_semantics=sem, vmem_limit_bytes=VMEM_LIMIT)


def _pick(n, pref):
    for c in pref:
        if n % c == 0:
            return c
    return n


def _kv_kernel(mem_ref, g_ref, w_ref, o_ref):
    mn = _rms(mem_ref[...], g_ref[...]).astype(BF16)
    o_ref[...] = _dot(mn, w_ref[...]).astype(BF16)


def _kv_call(mem2d, g_mem, w_ckv, n_mem):
    rows, d = mem2d.shape
    return pl.pallas_call(
        _kv_kernel,
        grid=(rows // n_mem,),
        in_specs=[pl.BlockSpec((n_mem, d), lambda i: (i, 0)),
                  pl.BlockSpec((1, d), lambda i: (0, 0)),
                  pl.BlockSpec((d, 2 * d), lambda i: (0, 0))],
        out_specs=pl.BlockSpec((n_mem, 2 * d), lambda i: (i, 0)),
        out_shape=jax.ShapeDtypeStruct((rows, 2 * d), BF16),
        compiler_params=_params(("arbitrary",)),
        name="kv",
    )(mem2d, g_mem, w_ckv)


SB_W = SB_HEADS * SB_HEAD_DIM
GDN_W = GDN_HEADS * GDN_HEAD_DIM
C_SB = (0, 3 * SB_W)
C_GQ = (C_SB[1], C_SB[1] + 3 * GDN_W)
C_GZ = (C_GQ[1], C_GQ[1] + GDN_W)


def _inproj_kernel(x_ref, g_ref, w_ref, sb_ref, gq_ref, gz_ref, gt_ref, ba_ref, *, d):
    xn = _rms(x_ref[...], g_ref[...]).astype(BF16)
    c_gt = (C_GZ[1], C_GZ[1] + 2 * d)
    c_ba = (c_gt[1], c_gt[1] + LANES)

    def mm(c):
        return _dot(xn, w_ref[:, c[0]:c[1]])

    sb_ref[...] = mm(C_SB).astype(BF16)
    gq_ref[...] = mm(C_GQ)
    gz_ref[...] = mm(C_GZ)
    gt_ref[...] = mm(c_gt).astype(BF16)
    ba_ref[...] = mm(c_ba)


def _inproj_call(x2d, g_mix, w_p, tm):
    t, d = x2d.shape
    n = w_p.shape[1]
    row = lambda i: (i, 0)
    fixed = lambda i: (0, 0)
    return pl.pallas_call(
        functools.partial(_inproj_kernel, d=d),
        grid=(t // tm,),
        in_specs=[pl.BlockSpec((tm, d), row),
                  pl.BlockSpec((1, d), fixed),
                  pl.BlockSpec((d, n), fixed)],
        out_specs=[pl.BlockSpec((tm, 3 * SB_W), row),
                   pl.BlockSpec((tm, 3 * GDN_W), row),
                   pl.BlockSpec((tm, GDN_W), row),
                   pl.BlockSpec((tm, 2 * d), row),
                   pl.BlockSpec((tm, LANES), row)],
        out_shape=[jax.ShapeDtypeStruct((t, 3 * SB_W), BF16),
                   jax.ShapeDtypeStruct((t, 3 * GDN_W), F32),
                   jax.ShapeDtypeStruct((t, GDN_W), F32),
                   jax.ShapeDtypeStruct((t, 2 * d), BF16),
                   jax.ShapeDtypeStruct((t, LANES), F32)],
        compiler_params=_params(("arbitrary",)),
        name="inproj",
    )(x2d, g_mix, w_p)


SB_GROUP = 2 * LANES
SB_HPG = SB_GROUP // SB_HEAD_DIM


def _sb_kernel(q_ref, k_ref, v_ref, o_ref, acc_ref, c_ref, *, seq, blk):
    scale = SB_HEAD_DIM ** -0.5
    lane = lax.broadcasted_iota(jnp.int32, (blk, SB_GROUP), 1)
    in_head = [jnp.logical_and(lane >= h * SB_HEAD_DIM, lane < (h + 1) * SB_HEAD_DIM)
               for h in range(SB_HPG)]
    r_i = lax.broadcasted_iota(jnp.int32, (blk, blk), 0)
    c_i = lax.broadcasted_iota(jnp.int32, (blk, blk), 1)
    causal = c_i < r_i
    jj = lax.broadcasted_iota(jnp.int32, (blk, 2 * blk), 0)
    ss = lax.broadcasted_iota(jnp.int32, (blk, 2 * blk), 1)
    m_ext = jnp.where(ss >= blk, 1.0, jnp.where(jj > ss, 1.0, 0.0)).astype(BF16)

    def all_heads(q, kblk, vblk, masked):
        zero = jnp.zeros_like(q)
        hs = range(SB_HPG)
        z = [_dot_nt(jnp.where(in_head[h], q, zero), kblk) * scale for h in hs]
        sp = [_softplus(z[h]) for h in hs]
        lk = [-sp[h] for h in hs]
        if masked:
            lk = [jnp.where(causal, lk[h], 0.0) for h in hs]
        parts = [_split2(lk[h]) for h in hs]
        ext = [_dot(parts[h][0], m_ext) + _dot(parts[h][1], m_ext) for h in hs]
        cm = None
        for h in hs:
            c = c_ref[h]
            a = jnp.exp((z[h] - sp[h]) + ext[h][:, :blk] + c)
            if masked:
                a = jnp.where(causal, a, 0.0)
            acc_ref[h] += _dot(a.astype(BF16), vblk)
            c_new = c + ext[h][:, blk:]
            c_ref[h] = c_new
            cm = c_new if cm is None else jnp.maximum(cm, c_new)
        return jnp.max(cm)

    def qblock(qi, carry):
        q0 = pl.multiple_of(qi * blk, blk)
        q = q_ref[pl.ds(q0, blk), :]
        acc_ref[...] = jnp.zeros_like(acc_ref)
        c_ref[...] = jnp.zeros_like(c_ref)
        m0 = all_heads(q, k_ref[pl.ds(q0, blk), :], v_ref[pl.ds(q0, blk), :], True)

        def cond(st):
            kb, cm = st
            return jnp.logical_and(kb >= 0, cm > SB_SKIP_LOG)

        def body(st):
            kb, _ = st
            k0 = pl.multiple_of(kb * blk, blk)
            return kb - 1, all_heads(q, k_ref[pl.ds(k0, blk), :], v_ref[pl.ds(k0, blk), :], False)

        lax.while_loop(cond, body, (qi - 1, m0))
        o = acc_ref[0]
        for h in range(1, SB_HPG):
            o = jnp.where(in_head[h], acc_ref[h], o)
        o_ref[pl.ds(q0, blk), :] = o.astype(BF16)
        return carry

    lax.fori_loop(0, seq // blk, qblock, 0)


def _sb_call(sbqkv, batch, seq):
    t = sbqkv.shape[0]
    ng = SB_W // SB_GROUP
    blk = _pick(seq, (256, 128))
    return pl.pallas_call(
        functools.partial(_sb_kernel, seq=seq, blk=blk),
        grid=(batch, ng),
        in_specs=[pl.BlockSpec((seq, SB_GROUP), lambda b, g: (b, g)),
                  pl.BlockSpec((seq, SB_GROUP), lambda b, g: (b, ng + g)),
                  pl.BlockSpec((seq, SB_GROUP), lambda b, g: (b, 2 * ng + g))],
        out_specs=pl.BlockSpec((seq, SB_GROUP), lambda b, g: (b, g)),
        out_shape=jax.ShapeDtypeStruct((t, SB_W), BF16),
        scratch_shapes=[pltpu.VMEM((SB_HPG, blk, SB_GROUP), F32),
                        pltpu.VMEM((SB_HPG, blk, blk), F32)],
        compiler_params=_params(("arbitrary", "arbitrary")),
        name="sb",
    )(sbqkv, sbqkv, sbqkv)


GDN_PAD = 8
GDN_BASE = 8


def _gdn_kernel(qkv_ref, z_ref, ba_ref, conv_ref, alog_ref, dtb_ref, ng_ref, o_ref,
                xbuf, u_s, w_s, qd_s, kd_s, qk_s, vn_s, os_, state, *, blk):
    hd = GDN_HEAD_DIM
    w3 = 3 * GDN_W
    nh = GDN_HEADS

    @pl.when(pl.program_id(1) == 0)
    def _():
        state[...] = jnp.zeros_like(state)
        xbuf[0:GDN_PAD, :] = jnp.zeros((GDN_PAD, w3), F32)

    xbuf[GDN_PAD:GDN_PAD + blk, :] = qkv_ref[...]
    cw = conv_ref[...]
    y = xbuf[GDN_PAD:GDN_PAD + blk, :] * cw[CONV_WIDTH - 1:CONV_WIDTH, :]
    for j in range(CONV_WIDTH - 1):
        d = CONV_WIDTH - 1 - j
        y = y + xbuf[GDN_PAD - d:GDN_PAD - d + blk, :] * cw[j:j + 1, :]
    xbuf[0:GDN_PAD, :] = xbuf[blk:blk + GDN_PAD, :]
    y = y * _sigmoid(y)

    ba = ba_ref[...]
    lane = lax.broadcasted_iota(jnp.int32, (blk, LANES), 1)
    g = -jnp.exp(alog_ref[...]) * _softplus(ba + dtb_ref[...])
    g = jnp.where(jnp.logical_and(lane >= nh, lane < 2 * nh), g, 0.0)
    rr = lax.broadcasted_iota(jnp.int32, (blk, blk), 0)
    cc = lax.broadcasted_iota(jnp.int32, (blk, blk), 1)
    same = (rr // CHUNK) == (cc // CHUNK)
    eye = jnp.where(rr == cc, 1.0, 0.0)
    bd_ones = jnp.where(same, 1.0, 0.0)
    bd_incl = jnp.where(cc <= rr, bd_ones, 0.0)
    bd_strict = bd_incl - eye
    gc = _dot_exact_lhs(bd_incl.astype(BF16), g)
    gl = _dot_exact_lhs(bd_ones.astype(BF16), g)
    eg = jnp.exp(gc)
    ek = jnp.exp(gl - gc)
    ec = jnp.exp(gl)
    beta = _sigmoid(ba)
    incl = bd_incl > 0.5

    heads = range(nh)
    a_s, rhs = [], []
    for h in heads:
        qh = y[:, h * hd:(h + 1) * hd]
        kh = y[:, GDN_W + h * hd:GDN_W + (h + 1) * hd]
        vh = y[:, 2 * GDN_W + h * hd:2 * GDN_W + (h + 1) * hd]
        q = qh * (lax.rsqrt(jnp.sum(qh * qh, axis=-1, keepdims=True) + EPS) * (hd ** -0.5))
        k = kh * lax.rsqrt(jnp.sum(kh * kh, axis=-1, keepdims=True) + EPS)
        gcol = gc[:, nh + h:nh + h + 1]
        grow = jnp.sum(eye * gcol, axis=0, keepdims=True)
        dm = jnp.exp(jnp.where(incl, gcol - grow, -jnp.inf))
        bcol = beta[:, h:h + 1]
        kb = k * bcol
        k16 = k.astype(BF16)
        a = _dot_nt(kb.astype(BF16), k16) * dm * bd_strict
        egc = eg[:, nh + h:nh + h + 1]
        qk_s[h] = (_dot_nt(q.astype(BF16), k16) * dm).astype(BF16)
        qd_s[h] = (q * egc).astype(BF16)
        kd_s[h] = (k * ek[:, nh + h:nh + h + 1]).astype(BF16)
        rhs.append(_split2(jnp.concatenate([vh * bcol, kb * egc], axis=1)))
        a_s.append(a)
    def mm(p, q):
        return _dot(p.astype(BF16), q.astype(BF16))

    in_base = jnp.where(rr // GDN_BASE == cc // GDN_BASE, 1.0, 0.0)
    xs = []
    for h in heads:
        n = -(a_s[h] * in_base)
        x = eye + n
        p = mm(n, n)
        w = 2
        while w < GDN_BASE:
            x = x + mm(x, p)
            w *= 2
            if w < GDN_BASE:
                p = mm(p, p)
        xs.append(x)
    w = GDN_BASE
    while w < CHUNK:
        m_off = jnp.where(jnp.logical_and(rr // w != cc // w, rr // (2 * w) == cc // (2 * w)), 1.0, 0.0)
        xs = [xs[h] - mm(mm(xs[h], a_s[h] * m_off), xs[h]) for h in heads]
        w *= 2
    for h in heads:
        sol = _dot_hp2(_split2(xs[h]), rhs[h])
        u_s[h] = sol[:, :hd]
        w_s[h] = sol[:, hd:].astype(BF16)

    vn_s[...] = jnp.zeros_like(vn_s)
    for c in range(blk // CHUNK):
        rows = slice(c * CHUNK, (c + 1) * CHUNK)
        for h in range(nh):
            s = state[h]
            s16 = s.astype(BF16)
            vnew = u_s[h, rows, :] - _dot(w_s[h, rows, :], s16)
            vn16 = vnew.astype(BF16)
            vn_s[h, rows, :] = vn16
            os_[rows, h * hd:(h + 1) * hd] = (_dot(qd_s[h, rows, :], s16)
                                              + _dot(qk_s[h, rows, :], vn_s[h]))
            cd = ec[c * CHUNK:c * CHUNK + 1, nh + h:nh + h + 1]
            state[h] = s * cd + _dot_tn(kd_s[h, rows, :], vn16)

    zg = z_ref[...]
    ng = ng_ref[...]
    for h in range(nh):
        cols = slice(h * hd, (h + 1) * hd)
        oh = _rms(os_[:, cols], ng)
        zh = zg[:, cols]
        o_ref[:, cols] = (oh * (zh * _sigmoid(zh))).astype(BF16)


def _gdn_call(gqkv, gz, gba, conv_w, alog_l, dtb_l, norm_g, batch, seq, blk):
    t = gqkv.shape[0]
    nb = seq // blk
    row = lambda b, i: (b * nb + i, 0)
    fixed = lambda b, i: (0, 0)
    w3 = 3 * GDN_W
    nh, hd = GDN_HEADS, GDN_HEAD_DIM
    return pl.pallas_call(
        functools.partial(_gdn_kernel, blk=blk),
        grid=(batch, nb),
        in_specs=[pl.BlockSpec((blk, w3), row),
                  pl.BlockSpec((blk, GDN_W), row),
                  pl.BlockSpec((blk, LANES), row),
                  pl.BlockSpec((CONV_WIDTH, w3), fixed),
                  pl.BlockSpec((1, LANES), fixed),
                  pl.BlockSpec((1, LANES), fixed),
                  pl.BlockSpec((1, hd), fixed)],
        out_specs=pl.BlockSpec((blk, GDN_W), row),
        out_shape=jax.ShapeDtypeStruct((t, GDN_W), BF16),
        scratch_shapes=[pltpu.VMEM((blk + GDN_PAD, w3), F32),
                        pltpu.VMEM((nh, blk, hd), F32),
                        pltpu.VMEM((nh, blk, hd), BF16),
                        pltpu.VMEM((nh, blk, hd), BF16),
                        pltpu.VMEM((nh, blk, hd), BF16),
                        pltpu.VMEM((nh, blk, blk), BF16),
                        pltpu.VMEM((nh, blk, hd), BF16),
                        pltpu.VMEM((blk, GDN_W), F32),
                        pltpu.VMEM((nh, hd, hd), F32)],
        compiler_params=_params(("arbitrary", "arbitrary")),
        name="gdn",
    )(gqkv, gz, gba, conv_w, alog_l, dtb_l, norm_g)


def _merge_kernel(x_ref, osb_ref, og_ref, gt_ref, kv_ref, wsb_ref, wg_ref, wmix_ref,
                  gcr_ref, wcq_ref, wco_ref, gpe_ref, x2_ref, xpt_ref, *, d):
    gates = gt_ref[...].astype(F32)
    merged = (_sigmoid(gates[:, :d]) * _dot(osb_ref[...], wsb_ref[...])
              + _sigmoid(gates[:, d:]) * _dot(og_ref[...], wg_ref[...]))
    x1 = x_ref[...] + _dot(merged.astype(BF16), wmix_ref[...])
    hn = _rms(x1, gcr_ref[...]).astype(BF16)
    q = _dot(hn, wcq_ref[...])
    xd = d // X_HEADS
    kv = kv_ref[...]
    hs = range(X_HEADS)
    sc = [_dot_nt(q[:, h * xd:(h + 1) * xd].astype(BF16), kv[:, h * xd:(h + 1) * xd]) * (xd ** -0.5)
          for h in hs]
    ex = [jnp.exp(sc[h] - jnp.max(sc[h], axis=-1, keepdims=True)) for h in hs]
    pr = [(ex[h] / jnp.sum(ex[h], axis=-1, keepdims=True)).astype(BF16) for h in hs]
    outs = [_dot(pr[h], kv[:, d + h * xd:d + (h + 1) * xd]) for h in hs]
    o = jnp.concatenate(outs, axis=-1).astype(BF16)
    x2 = x1 + _dot(o, wco_ref[...])
    x2_ref[...] = x2
    xpt_ref[...] = _rms(x2, gpe_ref[...]).T.astype(BF16)


def _merge_call(x2d, osb, og, gates, kv, wsb, wg, wmix, g_cross, wcq, wco, g_peer, seq, n_mem, tm):
    t, d = x2d.shape
    per_b = seq // tm
    row = lambda i: (i, 0)
    fixed = lambda i: (0, 0)
    return pl.pallas_call(
        functools.partial(_merge_kernel, d=d),
        grid=(t // tm,),
        in_specs=[pl.BlockSpec((tm, d), row),
                  pl.BlockSpec((tm, SB_W), row),
                  pl.BlockSpec((tm, GDN_W), row),
                  pl.BlockSpec((tm, 2 * d), row),
                  pl.BlockSpec((n_mem, 2 * d), lambda i: (i // per_b, 0)),
                  pl.BlockSpec((SB_W, d), fixed),
                  pl.BlockSpec((GDN_W, d), fixed),
                  pl.BlockSpec((d, d), fixed),
                  pl.BlockSpec((1, d), fixed),
                  pl.BlockSpec((d, d), fixed),
                  pl.BlockSpec((d, d), fixed),
                  pl.BlockSpec((1, d), fixed)],
        out_specs=[pl.BlockSpec((tm, d), row), pl.BlockSpec((d, tm), lambda i: (0, i))],
        out_shape=[jax.ShapeDtypeStruct((t, d), F32), jax.ShapeDtypeStruct((d, t), BF16)],
        compiler_params=_params(("arbitrary",)),
        name="merge",
    )(x2d, osb, og, gates, kv, wsb, wg, wmix, g_cross, wcq, wco, g_peer)


N_HP = 2 * PEER_HEADS
N_TOP = PEER_TOPK + 1
PAIR_ROWS = [min(N_TOP, N_TOP // (a + 1)) for a in range(N_TOP)]
N_SINGLE = sum(1 for n in PAIR_ROWS if n == 1)
N_CAND = 64
assert sum(PAIR_ROWS) <= N_CAND
TOP_PAD = -(-N_TOP // 8) * 8


def _sort_network(n):
    pairs = []
    p = 1
    while p < n:
        k = p
        while k >= 1:
            for j in range(k % p, n - k, 2 * k):
                for i in range(min(k, n - j - k)):
                    if (i + j) // (2 * p) == (i + j + k) // (2 * p):
                        pairs.append((i + j, i + j + k))
            k //= 2
        p *= 2
    return pairs


def _peersel_kernel(xpt_ref, wpqt_ref, sk_ref, r2_ref, b2_ref, n1_ref, a1_ref,
                    qps, sall, tops, cand, best):
    qpt = _dot(wpqt_ref[...], xpt_ref[...])
    for hp in range(N_HP):
        qps[hp] = qpt[hp * PEER_KEYS:(hp + 1) * PEER_KEYS, :].astype(BF16)

    def top_values(s, dst_ref):
        sub = s.shape[0] // SUBLANES
        v = [s[k * SUBLANES:(k + 1) * SUBLANES, :] for k in range(sub)]
        for a, b in _sort_network(sub):
            v[a], v[b] = jnp.maximum(v[a], v[b]), jnp.minimum(v[a], v[b])
        for r in range(N_TOP):
            m = jnp.max(v[0], axis=0, keepdims=True)
            dst_ref[r:r + 1, :] = m
            hit = v[0] == m
            for k in range(min(sub, N_TOP - r - 1)):
                nxt = v[k + 1] if k + 1 < sub else jnp.full_like(v[k], -jnp.inf)
                v[k] = jnp.where(hit, nxt, v[k])

    def head_scores(h, carry):
        for p in range(2):
            s = _dot(sk_ref[2 * h + p], qps[2 * h + p])
            sall[2 * h + p] = s
            top_values(s, tops.at[2 * h + p])
        return carry

    lax.fori_loop(0, PEER_HEADS, head_scores, 0)

    n_multi = N_TOP - N_SINGLE
    cand[...] = jnp.full(cand.shape, -jnp.inf, F32)

    def head(h, carry):
        xs = tops[2 * h, 0:N_TOP, :]
        ys = tops[2 * h + 1, 0:N_TOP, :]
        row = 0
        for a in range(n_multi):
            cand[row:row + PAIR_ROWS[a], :] = xs[a:a + 1, :] + ys[0:PAIR_ROWS[a], :]
            row += PAIR_ROWS[a]
        cand[row:row + N_SINGLE, :] = xs[n_multi:N_TOP, :] + ys[0:1, :]
        top_values(cand[...], best)
        bs = best[0:N_TOP, :]
        top = bs[0:PEER_TOPK, :]
        zsum = jnp.sum(jnp.exp(top - top[0:1, :]), axis=0, keepdims=True)
        theta = 0.5 * (bs[PEER_TOPK - 1:PEER_TOPK, :] + bs[PEER_TOPK:PEER_TOPK + 1, :])
        s1 = sall[2 * h]
        s2 = sall[2 * h + 1]
        thr = theta - s1
        rank = jnp.zeros_like(s2)
        count = jnp.zeros_like(s1)
        for b in range(N_TOP):
            yb = ys[b:b + 1, :]
            rank = jnp.where(yb > s2, b + 1.0, rank)
            count = jnp.where(yb >= thr, b + 1.0, count)
        rows = pl.ds(pl.multiple_of(h * PEER_KEYS, PEER_KEYS), PEER_KEYS)
        rows16 = pl.ds(pl.multiple_of(h * (PEER_KEYS // 2), PEER_KEYS // 2), PEER_KEYS // 2)
        r2_ref[rows16, :] = pltpu.bitcast(rank.astype(BF16), jnp.int32)
        b2_ref[rows16, :] = pltpu.bitcast(jnp.exp(s2 - ys[0:1, :]).astype(BF16), jnp.int32)
        n1_ref[rows, :] = count
        a1_ref[rows, :] = jnp.exp(s1 - xs[0:1, :]) / zsum
        return carry

    lax.fori_loop(0, PEER_HEADS, head, 0)


def _peersel_call(xpt, wpqt, sk, tm):
    d, t = xpt.shape
    nq = wpqt.shape[0]
    kd = sk.shape[-1]
    out32 = jax.ShapeDtypeStruct((PEER_HEADS * PEER_KEYS, t), F32)
    out16 = jax.ShapeDtypeStruct((PEER_HEADS * PEER_KEYS // 2, t), jnp.int32)
    ospec = pl.BlockSpec((PEER_HEADS * PEER_KEYS, tm), lambda i: (0, i))
    ospec16 = pl.BlockSpec((PEER_HEADS * PEER_KEYS // 2, tm), lambda i: (0, i))
    return pl.pallas_call(
        _peersel_kernel,
        grid=(t // tm,),
        in_specs=[pl.BlockSpec((d, tm), lambda i: (0, i)),
                  pl.BlockSpec((nq, d), lambda i: (0, 0)),
                  pl.BlockSpec((N_HP, PEER_KEYS, kd), lambda i: (0, 0, 0))],
        out_specs=[ospec16, ospec16, ospec, ospec],
        out_shape=[out16, out16, out32, out32],
        scratch_shapes=[pltpu.VMEM((N_HP, kd, tm), BF16),
                        pltpu.VMEM((N_HP, PEER_KEYS, tm), F32),
                        pltpu.VMEM((N_HP, TOP_PAD, tm), F32),
                        pltpu.VMEM((N_CAND, tm), F32),
                        pltpu.VMEM((TOP_PAD, tm), F32)],
        compiler_params=_params(("arbitrary",)),
        name="peersel",
    )(xpt, wpqt, sk)


PEER_EBLK = 16 * PEER_KEYS
PEER_GRP = 2 * PEER_KEYS
PEER_RC = 16


def _gelu(x):
    return 0.5 * x * (1.0 + lax.erf(x * 0.7071067811865476))


def _peer_kernel(xpt_ref, u_ref, vt_ref, r2_ref, b2_ref, n1_ref, a1_ref, x2_ref, gf_ref, o_ref,
                 ht, at, acc):
    e = pl.program_id(1)

    @pl.when(e == 0)
    def _():
        acc[...] = jnp.zeros_like(acc)

    n_i = PEER_EBLK // PEER_KEYS
    n_grp = PEER_EBLK // PEER_GRP

    def scores(g):
        rows = slice(g * PEER_GRP, (g + 1) * PEER_GRP)
        ht[rows, :] = _dot(u_ref[rows, :], xpt_ref[...])

    def gates(sub):
        i = n_i * e + sub
        for lt in range(ht.shape[1] // LANES):
            tl = slice(lt * LANES, (lt + 1) * LANES)
            def row16(ref, h):
                row = ref[pl.ds(h * PEER_KEYS + i, 1), :][:, tl]
                return jnp.broadcast_to(row, (PEER_RC, LANES)).astype(BF16)
            cnt = [row16(n1_ref, h) for h in range(PEER_HEADS)]
            cf = [row16(a1_ref, h) for h in range(PEER_HEADS)]
            zero = jnp.zeros((PEER_RC, LANES), BF16)
            for r in range(PEER_KEYS // PEER_RC):
                j0 = r * PEER_RC
                w = None
                for h in range(PEER_HEADS):
                    jr = slice((h * PEER_KEYS + j0) // 2, (h * PEER_KEYS + j0 + PEER_RC) // 2)
                    r2 = pltpu.bitcast(r2_ref[jr, tl], BF16)
                    b2 = pltpu.bitcast(b2_ref[jr, tl], BF16)
                    wh = jnp.where(r2 < cnt[h], b2 * cf[h], zero)
                    w = wh if w is None else w + wh
                row0 = sub * PEER_KEYS + j0
                at[row0:row0 + PEER_RC, tl] = _gelu(ht[row0:row0 + PEER_RC, tl]).astype(BF16) * w

    scores(0)
    for g in range(n_grp):
        if g + 1 < n_grp:
            scores(g + 1)
        for sub in range(g * PEER_GRP // PEER_KEYS, (g + 1) * PEER_GRP // PEER_KEYS):
            gates(sub)
        if g % 2 == 1:
            rows = slice((g - 1) * PEER_GRP, (g + 1) * PEER_GRP)
            acc[...] += _dot(vt_ref[:, rows], at[rows, :])

    @pl.when(e == pl.num_programs(1) - 1)
    def _():
        x3 = x2_ref[...] + acc[...].T
        o_ref[...] = _rms(x3, gf_ref[...])


def _peer_call(xpt, u16, vt16, r2, b2, n1, a1, x2, g_final, tb):
    d, t = xpt.shape
    n_exp = u16.shape[0]
    tok = lambda i, e: (i, 0)
    sel = pl.BlockSpec((PEER_HEADS * PEER_KEYS, tb), lambda i, e: (0, i))
    sel16 = pl.BlockSpec((PEER_HEADS * PEER_KEYS // 2, tb), lambda i, e: (0, i))
    return pl.pallas_call(
        _peer_kernel,
        grid=(t // tb, n_exp // PEER_EBLK),
        in_specs=[pl.BlockSpec((d, tb), lambda i, e: (0, i)),
                  pl.BlockSpec((PEER_EBLK, d), lambda i, e: (e, 0)),
                  pl.BlockSpec((d, PEER_EBLK), lambda i, e: (0, e)),
                  sel16, sel16, sel, sel,
                  pl.BlockSpec((tb, d), tok),
                  pl.BlockSpec((1, d), lambda i, e: (0, 0))],
        out_specs=pl.BlockSpec((tb, d), tok),
        out_shape=jax.ShapeDtypeStruct((t, d), F32),
        scratch_shapes=[pltpu.VMEM((PEER_EBLK, tb), F32),
                        pltpu.VMEM((PEER_EBLK, tb), BF16),
                        pltpu.VMEM((d, tb), F32)],
        compiler_params=_params(("arbitrary", "arbitrary")),
        name="peer",
    )(xpt, u16, vt16, r2, b2, n1, a1, x2, g_final)


def _regroup_w_in(w_in):
    o_ba = 3 * SB_W + 4 * GDN_W
    ba = jnp.pad(w_in[:, o_ba:o_ba + 2 * GDN_HEADS], ((0, 0), (0, LANES - 2 * GDN_HEADS)))
    return jnp.concatenate([w_in[:, :o_ba], w_in[:, o_ba + 2 * GDN_HEADS:], ba], axis=1).astype(BF16)


def _layer(x2d, kv, batch, seq, n_mem, g_mix, w_in, gdn_conv, gdn_a_log, gdn_dt_bias, gdn_norm_g,
           w_sb_up, w_gdn_up, w_mix_out, g_cross, w_cq, w_co, g_peer, w_pq, peer_subkeys,
           peer_u, peer_v, g_out):
    t, d = x2d.shape
    tm = _pick(seq, (256, 128))
    sbqkv, gqkv, gz, gates, gba = _inproj_call(x2d, g_mix.reshape(1, d), _regroup_w_in(w_in), tm)
    o_sb = _sb_call(sbqkv, batch, seq)
    lane_pad = lambda v: jnp.pad(v.astype(F32), (GDN_HEADS, LANES - 2 * GDN_HEADS)).reshape(1, LANES)
    o_g = _gdn_call(gqkv, gz, gba, gdn_conv, lane_pad(gdn_a_log), lane_pad(gdn_dt_bias),
                    gdn_norm_g.reshape(1, GDN_HEAD_DIM), batch, seq, _pick(seq, (256, 128, 64)))
    x2, xpt = _merge_call(x2d, o_sb, o_g, gates, kv, w_sb_up.astype(BF16), w_gdn_up.astype(BF16),
                          w_mix_out.astype(BF16), g_cross.reshape(1, d), w_cq.astype(BF16),
                          w_co.astype(BF16), g_peer.reshape(1, d), seq, n_mem, tm)
    sk = peer_subkeys.reshape(N_HP, PEER_KEYS, -1).astype(BF16)
    r2, b2, n1, a1 = _peersel_call(xpt, w_pq.T.astype(BF16), sk, tm)
    return _peer_call(xpt, peer_u.astype(BF16), peer_v.T.astype(BF16), r2, b2, n1, a1, x2,
                      g_out.reshape(1, d), _pick(t, (512, 256, 128)))


def kernel(x, mem, g_mix, w_in, gdn_conv, gdn_a_log, gdn_dt_bias, gdn_norm_g, w_sb_up, w_gdn_up,
           w_mix_out, g_cross, g_mem, w_cq, w_ckv, w_co, g_peer, w_pq, peer_subkeys, peer_u, peer_v,
           g_final):
    batch, seq, d = x.shape
    n_mem = mem.shape[1]
    depth = g_mix.shape[0]
    assert depth == 1, "the final RMSNorm is fused into the last layer's PEER kernel"
    x2d = x.reshape(batch * seq, d)
    mem2d = mem.reshape(batch * n_mem, d)
    l = 0
    kv = _kv_call(mem2d, g_mem[l].reshape(1, d), w_ckv[l].astype(BF16), n_mem)
    out = _layer(x2d, kv, batch, seq, n_mem, g_mix[l], w_in[l], gdn_conv[l], gdn_a_log[l],
                 gdn_dt_bias[l], gdn_norm_g[l], w_sb_up[l], w_gdn_up[l], w_mix_out[l], g_cross[l],
                 w_cq[l], w_co[l], g_peer[l], w_pq[l], peer_subkeys[l], peer_u[l], peer_v[l], g_final)
    return out.reshape(batch, seq, d)
```

```python
import functools

import jax
import jax.numpy as jnp
from jax import lax
from jax.experimental import pallas as pl
from jax.experimental.pallas import tpu as pltpu

F32 = jnp.float32
BF16 = jnp.bfloat16

EPS = 1e-6
SB_HEADS = 8
SB_HEAD_DIM = 64
GDN_HEADS = 4
GDN_HEAD_DIM = 128
CONV_WIDTH = 4
CHUNK = 64
X_HEADS = 4
PEER_HEADS = 8
PEER_KEYS = 128
PEER_TOPK = 16

LANES = 128
SUBLANES = 8
VMEM_LIMIT = 56 * 1024 * 1024
SB_SKIP_LOG = -104.0


def _dot(a, b):
    return jnp.dot(a, b, preferred_element_type=F32)


def _dot_nt(a, b):
    return lax.dot_general(a, b, (((1,), (1,)), ((), ())), preferred_element_type=F32)


def _dot_tn(a, b):
    return lax.dot_general(a, b, (((0,), (0,)), ((), ())), preferred_element_type=F32)


def _split3(a):
    hi = a.astype(BF16)
    r1 = a - hi.astype(F32)
    mid = r1.astype(BF16)
    lo = (r1 - mid.astype(F32)).astype(BF16)
    return hi, mid, lo


def _dot_exact_lhs(m_bf16, b):
    hi, mid, lo = _split3(b)
    return _dot(m_bf16, hi) + _dot(m_bf16, mid) + _dot(m_bf16, lo)


def _split2(a):
    hi = a.astype(BF16)
    return hi, (a - hi.astype(F32)).astype(BF16)


def _dot_hp2(a2, b2):
    return _dot(a2[0], b2[0]) + _dot(a2[0], b2[1]) + _dot(a2[1], b2[0])


def _rms(x, g):
    return x * lax.rsqrt(jnp.mean(x * x, axis=-1, keepdims=True) + EPS) * g


def _softplus(x):
    return jnp.maximum(x, 0.0) + jnp.log1p(jnp.exp(-jnp.abs(x)))


def _sigmoid(x):
    return 1.0 / (1.0 + jnp.exp(-x))


def _params(sem):
    return pltpu.CompilerParams(dimension_semantics=sem, vmem_limit_bytes=VMEM_LIMIT)


def _pick(n, pref):
    for c in pref:
        if n % c == 0:
            return c
    return n


def _kv_kernel(mem_ref, g_ref, w_ref, o_ref):
    mn = _rms(mem_ref[...], g_ref[...]).astype(BF16)
    o_ref[...] = _dot(mn, w_ref[...]).astype(BF16)


def _kv_call(mem2d, g_mem, w_ckv, n_mem):
    rows, d = mem2d.shape
    return pl.pallas_call(
        _kv_kernel,
        grid=(rows // n_mem,),
        in_specs=[pl.BlockSpec((n_mem, d), lambda i: (i, 0)),
                  pl.BlockSpec((1, d), lambda i: (0, 0)),
                  pl.BlockSpec((d, 2 * d), lambda i: (0, 0))],
        out_specs=pl.BlockSpec((n_mem, 2 * d), lambda i: (i, 0)),
        out_shape=jax.ShapeDtypeStruct((rows, 2 * d), BF16),
        compiler_params=_params(("arbitrary",)),
        name="kv",
    )(mem2d, g_mem, w_ckv)


SB_W = SB_HEADS * SB_HEAD_DIM
GDN_W = GDN_HEADS * GDN_HEAD_DIM
C_SB = (0, 3 * SB_W)
C_GQ = (C_SB[1], C_SB[1] + 3 * GDN_W)
C_GZ = (C_GQ[1], C_GQ[1] + GDN_W)


def _inproj_kernel(x_ref, g_ref, w_ref, sb_ref, gq_ref, gz_ref, gt_ref, ba_ref, *, d):
    xn = _rms(x_ref[...], g_ref[...]).astype(BF16)
    c_gt = (C_GZ[1], C_GZ[1] + 2 * d)
    c_ba = (c_gt[1], c_gt[1] + LANES)

    def mm(c):
        return _dot(xn, w_ref[:, c[0]:c[1]])

    sb_ref[...] = mm(C_SB).astype(BF16)
    gq_ref[...] = mm(C_GQ)
    gz_ref[...] = mm(C_GZ)
    gt_ref[...] = mm(c_gt).astype(BF16)
    ba_ref[...] = mm(c_ba)


def _inproj_call(x2d, g_mix, w_p, tm):
    t, d = x2d.shape
    n = w_p.shape[1]
    row = lambda i: (i, 0)
    fixed = lambda i: (0, 0)
    return pl.pallas_call(
        functools.partial(_inproj_kernel, d=d),
        grid=(t // tm,),
        in_specs=[pl.BlockSpec((tm, d), row),
                  pl.BlockSpec((1, d), fixed),
                  pl.BlockSpec((d, n), fixed)],
        out_specs=[pl.BlockSpec((tm, 3 * SB_W), row),
                   pl.BlockSpec((tm, 3 * GDN_W), row),
                   pl.BlockSpec((tm, GDN_W), row),
                   pl.BlockSpec((tm, 2 * d), row),
                   pl.BlockSpec((tm, LANES), row)],
        out_shape=[jax.ShapeDtypeStruct((t, 3 * SB_W), BF16),
                   jax.ShapeDtypeStruct((t, 3 * GDN_W), F32),
                   jax.ShapeDtypeStruct((t, GDN_W), F32),
                   jax.ShapeDtypeStruct((t, 2 * d), BF16),
                   jax.ShapeDtypeStruct((t, LANES), F32)],
        compiler_params=_params(("arbitrary",)),
        name="inproj",
    )(x2d, g_mix, w_p)


SB_GROUP = 2 * LANES
SB_HPG = SB_GROUP // SB_HEAD_DIM


def _sb_kernel(q_ref, k_ref, v_ref, o_ref, acc_ref, c_ref, *, seq, blk):
    scale = SB_HEAD_DIM ** -0.5
    lane = lax.broadcasted_iota(jnp.int32, (blk, SB_GROUP), 1)
    in_head = [jnp.logical_and(lane >= h * SB_HEAD_DIM, lane < (h + 1) * SB_HEAD_DIM)
               for h in range(SB_HPG)]
    r_i = lax.broadcasted_iota(jnp.int32, (blk, blk), 0)
    c_i = lax.broadcasted_iota(jnp.int32, (blk, blk), 1)
    causal = c_i < r_i
    jj = lax.broadcasted_iota(jnp.int32, (blk, 2 * blk), 0)
    ss = lax.broadcasted_iota(jnp.int32, (blk, 2 * blk), 1)
    m_ext = jnp.where(ss >= blk, 1.0, jnp.where(jj > ss, 1.0, 0.0)).astype(BF16)

    def all_heads(q, kblk, vblk, masked):
        zero = jnp.zeros_like(q)
        hs = range(SB_HPG)
        z = [_dot_nt(jnp.where(in_head[h], q, zero), kblk) * scale for h in hs]
        sp = [_softplus(z[h]) for h in hs]
        lk = [-sp[h] for h in hs]
        if masked:
            lk = [jnp.where(causal, lk[h], 0.0) for h in hs]
        parts = [_split2(lk[h]) for h in hs]
        ext = [_dot(parts[h][0], m_ext) + _dot(parts[h][1], m_ext) for h in hs]
        cm = None
        for h in hs:
            c = c_ref[h]
            a = jnp.exp((z[h] - sp[h]) + ext[h][:, :blk] + c)
            if masked:
                a = jnp.where(causal, a, 0.0)
            acc_ref[h] += _dot(a.astype(BF16), vblk)
            c_new = c + ext[h][:, blk:]
            c_ref[h] = c_new
            cm = c_new if cm is None else jnp.maximum(cm, c_new)
        return jnp.max(cm)

    def qblock(qi, carry):
        q0 = pl.multiple_of(qi * blk, blk)
        q = q_ref[pl.ds(q0, blk), :]
        acc_ref[...] = jnp.zeros_like(acc_ref)
        c_ref[...] = jnp.zeros_like(c_ref)
        m0 = all_heads(q, k_ref[pl.ds(q0, blk), :], v_ref[pl.ds(q0, blk), :], True)

        def cond(st):
            kb, cm = st
            return jnp.logical_and(kb >= 0, cm > SB_SKIP_LOG)

        def body(st):
            kb, _ = st
            k0 = pl.multiple_of(kb * blk, blk)
            return kb - 1, all_heads(q, k_ref[pl.ds(k0, blk), :], v_ref[pl.ds(k0, blk), :], False)

        lax.while_loop(cond, body, (qi - 1, m0))
        o = acc_ref[0]
        for h in range(1, SB_HPG):
            o = jnp.where(in_head[h], acc_ref[h], o)
        o_ref[pl.ds(q0, blk), :] = o.astype(BF16)
        return carry

    lax.fori_loop(0, seq // blk, qblock, 0)


def _sb_call(sbqkv, batch, seq):
    t = sbqkv.shape[0]
    ng = SB_W // SB_GROUP
    blk = _pick(seq, (256, 128))
    return pl.pallas_call(
        functools.partial(_sb_kernel, seq=seq, blk=blk),
        grid=(batch, ng),
        in_specs=[pl.BlockSpec((seq, SB_GROUP), lambda b, g: (b, g)),
                  pl.BlockSpec((seq, SB_GROUP), lambda b, g: (b, ng + g)),
                  pl.BlockSpec((seq, SB_GROUP), lambda b, g: (b, 2 * ng + g))],
        out_specs=pl.BlockSpec((seq, SB_GROUP), lambda b, g: (b, g)),
        out_shape=jax.ShapeDtypeStruct((t, SB_W), BF16),
        scratch_shapes=[pltpu.VMEM((SB_HPG, blk, SB_GROUP), F32),
                        pltpu.VMEM((SB_HPG, blk, blk), F32)],
        compiler_params=_params(("arbitrary", "arbitrary")),
        name="sb",
    )(sbqkv, sbqkv, sbqkv)


GDN_PAD = 8
GDN_BASE = 8


def _gdn_kernel(qkv_ref, z_ref, ba_ref, conv_ref, alog_ref, dtb_ref, ng_ref, o_ref,
                xbuf, u_s, w_s, qd_s, kd_s, qk_s, vn_s, os_, state, *, blk):
    hd = GDN_HEAD_DIM
    w3 = 3 * GDN_W
    nh = GDN_HEADS

    @pl.when(pl.program_id(1) == 0)
    def _():
        state[...] = jnp.zeros_like(state)
        xbuf[0:GDN_PAD, :] = jnp.zeros((GDN_PAD, w3), F32)

    xbuf[GDN_PAD:GDN_PAD + blk, :] = qkv_ref[...]
    cw = conv_ref[...]
    y = xbuf[GDN_PAD:GDN_PAD + blk, :] * cw[CONV_WIDTH - 1:CONV_WIDTH, :]
    for j in range(CONV_WIDTH - 1):
        d = CONV_WIDTH - 1 - j
        y = y + xbuf[GDN_PAD - d:GDN_PAD - d + blk, :] * cw[j:j + 1, :]
    xbuf[0:GDN_PAD, :] = xbuf[blk:blk + GDN_PAD, :]
    y = y * _sigmoid(y)

    ba = ba_ref[...]
    lane = lax.broadcasted_iota(jnp.int32, (blk, LANES), 1)
    g = -jnp.exp(alog_ref[...]) * _softplus(ba + dtb_ref[...])
    g = jnp.where(jnp.logical_and(lane >= nh, lane < 2 * nh), g, 0.0)
    rr = lax.broadcasted_iota(jnp.int32, (blk, blk), 0)
    cc = lax.broadcasted_iota(jnp.int32, (blk, blk), 1)
    same = (rr // CHUNK) == (cc // CHUNK)
    eye = jnp.where(rr == cc, 1.0, 0.0)
    bd_ones = jnp.where(same, 1.0, 0.0)
    bd_incl = jnp.where(cc <= rr, bd_ones, 0.0)
    bd_strict = bd_incl - eye
    gc = _dot_exact_lhs(bd_incl.astype(BF16), g)
    gl = _dot_exact_lhs(bd_ones.astype(BF16), g)
    eg = jnp.exp(gc)
    ek = jnp.exp(gl - gc)
    ec = jnp.exp(gl)
    beta = _sigmoid(ba)
    incl = bd_incl > 0.5

    heads = range(nh)
    a_s, rhs = [], []
    for h in heads:
        qh = y[:, h * hd:(h + 1) * hd]
        kh = y[:, GDN_W + h * hd:GDN_W + (h + 1) * hd]
        vh = y[:, 2 * GDN_W + h * hd:2 * GDN_W + (h + 1) * hd]
        q = qh * (lax.rsqrt(jnp.sum(qh * qh, axis=-1, keepdims=True) + EPS) * (hd ** -0.5))
        k = kh * lax.rsqrt(jnp.sum(kh * kh, axis=-1, keepdims=True) + EPS)
        gcol = gc[:, nh + h:nh + h + 1]
        grow = jnp.sum(eye * gcol, axis=0, keepdims=True)
        dm = jnp.exp(jnp.where(incl, gcol - grow, -jnp.inf))
        bcol = beta[:, h:h + 1]
        kb = k * bcol
        k16 = k.astype(BF16)
        a = _dot_nt(kb.astype(BF16), k16) * dm * bd_strict
        egc = eg[:, nh + h:nh + h + 1]
        qk_s[h] = (_dot_nt(q.astype(BF16), k16) * dm).astype(BF16)
        qd_s[h] = (q * egc).astype(BF16)
        kd_s[h] = (k * ek[:, nh + h:nh + h + 1]).astype(BF16)
        rhs.append(_split2(jnp.concatenate([vh * bcol, kb * egc], axis=1)))
        a_s.append(a)
    def mm(p, q):
        return _dot(p.astype(BF16), q.astype(BF16))

    in_base = jnp.where(rr // GDN_BASE == cc // GDN_BASE, 1.0, 0.0)
    xs = []
    for h in heads:
        n = -(a_s[h] * in_base)
        x = eye + n
        p = mm(n, n)
        w = 2
        while w < GDN_BASE:
            x = x + mm(x, p)
            w *= 2
            if w < GDN_BASE:
                p = mm(p, p)
        xs.append(x)
    w = GDN_BASE
    while w < CHUNK:
        m_off = jnp.where(jnp.logical_and(rr // w != cc // w, rr // (2 * w) == cc // (2 * w)), 1.0, 0.0)
        xs = [xs[h] - mm(mm(xs[h], a_s[h] * m_off), xs[h]) for h in heads]
        w *= 2
    for h in heads:
        sol = _dot_hp2(_split2(xs[h]), rhs[h])
        u_s[h] = sol[:, :hd]
        w_s[h] = sol[:, hd:].astype(BF16)

    vn_s[...] = jnp.zeros_like(vn_s)
    for c in range(blk // CHUNK):
        rows = slice(c * CHUNK, (c + 1) * CHUNK)
        for h in range(nh):
            s = state[h]
            s16 = s.astype(BF16)
            vnew = u_s[h, rows, :] - _dot(w_s[h, rows, :], s16)
            vn16 = vnew.astype(BF16)
            vn_s[h, rows, :] = vn16
            os_[rows, h * hd:(h + 1) * hd] = (_dot(qd_s[h, rows, :], s16)
                                              + _dot(qk_s[h, rows, :], vn_s[h]))
            cd = ec[c * CHUNK:c * CHUNK + 1, nh + h:nh + h + 1]
            state[h] = s * cd + _dot_tn(kd_s[h, rows, :], vn16)

    zg = z_ref[...]
    ng = ng_ref[...]
    for h in range(nh):
        cols = slice(h * hd, (h + 1) * hd)
        oh = _rms(os_[:, cols], ng)
        zh = zg[:, cols]
        o_ref[:, cols] = (oh * (zh * _sigmoid(zh))).astype(BF16)


def _gdn_call(gqkv, gz, gba, conv_w, alog_l, dtb_l, norm_g, batch, seq, blk):
    t = gqkv.shape[0]
    nb = seq // blk
    row = lambda b, i: (b * nb + i, 0)
    fixed = lambda b, i: (0, 0)
    w3 = 3 * GDN_W
    nh, hd = GDN_HEADS, GDN_HEAD_DIM
    return pl.pallas_call(
        functools.partial(_gdn_kernel, blk=blk),
        grid=(batch, nb),
        in_specs=[pl.BlockSpec((blk, w3), row),
                  pl.BlockSpec((blk, GDN_W), row),
                  pl.BlockSpec((blk, LANES), row),
                  pl.BlockSpec((CONV_WIDTH, w3), fixed),
                  pl.BlockSpec((1, LANES), fixed),
                  pl.BlockSpec((1, LANES), fixed),
                  pl.BlockSpec((1, hd), fixed)],
        out_specs=pl.BlockSpec((blk, GDN_W), row),
        out_shape=jax.ShapeDtypeStruct((t, GDN_W), BF16),
        scratch_shapes=[pltpu.VMEM((blk + GDN_PAD, w3), F32),
                        pltpu.VMEM((nh, blk, hd), F32),
                        pltpu.VMEM((nh, blk, hd), BF16),
                        pltpu.VMEM((nh, blk, hd), BF16),
                        pltpu.VMEM((nh, blk, hd), BF16),
                        pltpu.VMEM((nh, blk, blk), BF16),
                        pltpu.VMEM((nh, blk, hd), BF16),
                        pltpu.VMEM((blk, GDN_W), F32),
                        pltpu.VMEM((nh, hd, hd), F32)],
        compiler_params=_params(("arbitrary", "arbitrary")),
        name="gdn",
    )(gqkv, gz, gba, conv_w, alog_l, dtb_l, norm_g)


def _merge_kernel(x_ref, osb_ref, og_ref, gt_ref, kv_ref, wsb_ref, wg_ref, wmix_ref,
                  gcr_ref, wcq_ref, wco_ref, gpe_ref, x2_ref, xpt_ref, *, d):
    gates = gt_ref[...].astype(F32)
    merged = (_sigmoid(gates[:, :d]) * _dot(osb_ref[...], wsb_ref[...])
              + _sigmoid(gates[:, d:]) * _dot(og_ref[...], wg_ref[...]))
    x1 = x_ref[...] + _dot(merged.astype(BF16), wmix_ref[...])
    hn = _rms(x1, gcr_ref[...]).astype(BF16)
    q = _dot(hn, wcq_ref[...])
    xd = d // X_HEADS
    kv = kv_ref[...]
    hs = range(X_HEADS)
    sc = [_dot_nt(q[:, h * xd:(h + 1) * xd].astype(BF16), kv[:, h * xd:(h + 1) * xd]) * (xd ** -0.5)
          for h in hs]
    ex = [jnp.exp(sc[h] - jnp.max(sc[h], axis=-1, keepdims=True)) for h in hs]
    pr = [(ex[h] / jnp.sum(ex[h], axis=-1, keepdims=True)).astype(BF16) for h in hs]
    outs = [_dot(pr[h], kv[:, d + h * xd:d + (h + 1) * xd]) for h in hs]
    o = jnp.concatenate(outs, axis=-1).astype(BF16)
    x2 = x1 + _dot(o, wco_ref[...])
    x2_ref[...] = x2
    xpt_ref[...] = _rms(x2, gpe_ref[...]).T.astype(BF16)


def _merge_call(x2d, osb, og, gates, kv, wsb, wg, wmix, g_cross, wcq, wco, g_peer, seq, n_mem, tm):
    t, d = x2d.shape
    per_b = seq // tm
    row = lambda i: (i, 0)
    fixed = lambda i: (0, 0)
    return pl.pallas_call(
        functools.partial(_merge_kernel, d=d),
        grid=(t // tm,),
        in_specs=[pl.BlockSpec((tm, d), row),
                  pl.BlockSpec((tm, SB_W), row),
                  pl.BlockSpec((tm, GDN_W), row),
                  pl.BlockSpec((tm, 2 * d), row),
                  pl.BlockSpec((n_mem, 2 * d), lambda i: (i // per_b, 0)),
                  pl.BlockSpec((SB_W, d), fixed),
                  pl.BlockSpec((GDN_W, d), fixed),
                  pl.BlockSpec((d, d), fixed),
                  pl.BlockSpec((1, d), fixed),
                  pl.BlockSpec((d, d), fixed),
                  pl.BlockSpec((d, d), fixed),
                  pl.BlockSpec((1, d), fixed)],
        out_specs=[pl.BlockSpec((tm, d), row), pl.BlockSpec((d, tm), lambda i: (0, i))],
        out_shape=[jax.ShapeDtypeStruct((t, d), F32), jax.ShapeDtypeStruct((d, t), BF16)],
        compiler_params=_params(("arbitrary",)),
        name="merge",
    )(x2d, osb, og, gates, kv, wsb, wg, wmix, g_cross, wcq, wco, g_peer)


N_HP = 2 * PEER_HEADS
N_TOP = PEER_TOPK + 1
PAIR_ROWS = [min(N_TOP, N_TOP // (a + 1)) for a in range(N_TOP)]
N_SINGLE = sum(1 for n in PAIR_ROWS if n == 1)
N_CAND = 64
assert sum(PAIR_ROWS) <= N_CAND
TOP_PAD = -(-N_TOP // 8) * 8


def _sort_network(n):
    pairs = []
    p = 1
    while p < n:
        k = p
        while k >= 1:
            for j in range(k % p, n - k, 2 * k):
                for i in range(min(k, n - j - k)):
                    if (i + j) // (2 * p) == (i + j + k) // (2 * p):
                        pairs.append((i + j, i + j + k))
            k //= 2
        p *= 2
    return pairs


def _peersel_kernel(xpt_ref, wpqt_ref, sk_ref, r2_ref, b2_ref, n1_ref, a1_ref,
                    qps, sall, tops, cand, best):
    qpt = _dot(wpqt_ref[...], xpt_ref[...])
    for hp in range(N_HP):
        qps[hp] = qpt[hp * PEER_KEYS:(hp + 1) * PEER_KEYS, :].astype(BF16)

    def top_values(s, dst_ref):
        sub = s.shape[0] // SUBLANES
        v = [s[k * SUBLANES:(k + 1) * SUBLANES, :] for k in range(sub)]
        for a, b in _sort_network(sub):
            v[a], v[b] = jnp.maximum(v[a], v[b]), jnp.minimum(v[a], v[b])
        for r in range(N_TOP):
            m = jnp.max(v[0], axis=0, keepdims=True)
            dst_ref[r:r + 1, :] = m
            hit = v[0] == m
            for k in range(min(sub, N_TOP - r - 1)):
                nxt = v[k + 1] if k + 1 < sub else jnp.full_like(v[k], -jnp.inf)
                v[k] = jnp.where(hit, nxt, v[k])

    def head_scores(h, carry):
        for p in range(2):
            s = _dot(sk_ref[2 * h + p], qps[2 * h + p])
            sall[2 * h + p] = s
            top_values(s, tops.at[2 * h + p])
        return carry

    lax.fori_loop(0, PEER_HEADS, head_scores, 0)

    n_multi = N_TOP - N_SINGLE
    cand[...] = jnp.full(cand.shape, -jnp.inf, F32)

    def head(h, carry):
        xs = tops[2 * h, 0:N_TOP, :]
        ys = tops[2 * h + 1, 0:N_TOP, :]
        row = 0
        for a in range(n_multi):
            cand[row:row + PAIR_ROWS[a], :] = xs[a:a + 1, :] + ys[0:PAIR_ROWS[a], :]
            row += PAIR_ROWS[a]
        cand[row:row + N_SINGLE, :] = xs[n_multi:N_TOP, :] + ys[0:1, :]
        top_values(cand[...], best)
        bs = best[0:N_TOP, :]
        top = bs[0:PEER_TOPK, :]
        zsum = jnp.sum(jnp.exp(top - top[0:1, :]), axis=0, keepdims=True)
        theta = 0.5 * (bs[PEER_TOPK - 1:PEER_TOPK, :] + bs[PEER_TOPK:PEER_TOPK + 1, :])
        s1 = sall[2 * h]
        s2 = sall[2 * h + 1]
        thr = theta - s1
        rank = jnp.zeros_like(s2)
        count = jnp.zeros_like(s1)
        for b in range(N_TOP):
            yb = ys[b:b + 1, :]
            rank = jnp.where(yb > s2, b + 1.0, rank)
            count = jnp.where(yb >= thr, b + 1.0, count)
        rows = pl.ds(pl.multiple_of(h * PEER_KEYS, PEER_KEYS), PEER_KEYS)
        rows16 = pl.ds(pl.multiple_of(h * (PEER_KEYS // 2), PEER_KEYS // 2), PEER_KEYS // 2)
        r2_ref[rows16, :] = pltpu.bitcast(rank.astype(BF16), jnp.int32)
        b2_ref[rows16, :] = pltpu.bitcast(jnp.exp(s2 - ys[0:1, :]).astype(BF16), jnp.int32)
        n1_ref[rows, :] = count
        a1_ref[rows, :] = jnp.exp(s1 - xs[0:1, :]) / zsum
        return carry

    lax.fori_loop(0, PEER_HEADS, head, 0)


def _peersel_call(xpt, wpqt, sk, tm):
    d, t = xpt.shape
    nq = wpqt.shape[0]
    kd = sk.shape[-1]
    out32 = jax.ShapeDtypeStruct((PEER_HEADS * PEER_KEYS, t), F32)
    out16 = jax.ShapeDtypeStruct((PEER_HEADS * PEER_KEYS // 2, t), jnp.int32)
    ospec = pl.BlockSpec((PEER_HEADS * PEER_KEYS, tm), lambda i: (0, i))
    ospec16 = pl.BlockSpec((PEER_HEADS * PEER_KEYS // 2, tm), lambda i: (0, i))
    return pl.pallas_call(
        _peersel_kernel,
        grid=(t // tm,),
        in_specs=[pl.BlockSpec((d, tm), lambda i: (0, i)),
                  pl.BlockSpec((nq, d), lambda i: (0, 0)),
                  pl.BlockSpec((N_HP, PEER_KEYS, kd), lambda i: (0, 0, 0))],
        out_specs=[ospec16, ospec16, ospec, ospec],
        out_shape=[out16, out16, out32, out32],
        scratch_shapes=[pltpu.VMEM((N_HP, kd, tm), BF16),
                        pltpu.VMEM((N_HP, PEER_KEYS, tm), F32),
                        pltpu.VMEM((N_HP, TOP_PAD, tm), F32),
                        pltpu.VMEM((N_CAND, tm), F32),
                        pltpu.VMEM((TOP_PAD, tm), F32)],
        compiler_params=_params(("arbitrary",)),
        name="peersel",
    )(xpt, wpqt, sk)


PEER_EBLK = 16 * PEER_KEYS
PEER_GRP = 2 * PEER_KEYS
PEER_SPLIT = 4
PEER_SLAB = PEER_EBLK // PEER_SPLIT
assert PEER_SLAB % (2 * PEER_GRP) == 0
PEER_RC = 16


def _gelu(x):
    return 0.5 * x * (1.0 + lax.erf(x * 0.7071067811865476))


def _peer_kernel(xpt_ref, *refs):
    u_refs = refs[:PEER_SPLIT]
    vt_refs = refs[PEER_SPLIT:2 * PEER_SPLIT]
    r2_ref, b2_ref, n1_ref, a1_ref, x2_ref, gf_ref, o_ref, ht, at, acc = refs[2 * PEER_SPLIT:]
    e = pl.program_id(1)

    @pl.when(e == 0)
    def _():
        acc[...] = jnp.zeros_like(acc)

    n_i = PEER_EBLK // PEER_KEYS
    n_grp = PEER_EBLK // PEER_GRP

    def scores(g):
        rows = slice(g * PEER_GRP, (g + 1) * PEER_GRP)
        k, r0 = divmod(g * PEER_GRP, PEER_SLAB)
        ht[rows, :] = _dot(u_refs[k][r0:r0 + PEER_GRP, :], xpt_ref[...])

    def gates(sub):
        i = n_i * e + sub
        for lt in range(ht.shape[1] // LANES):
            tl = slice(lt * LANES, (lt + 1) * LANES)
            def row16(ref, h):
                row = ref[pl.ds(h * PEER_KEYS + i, 1), :][:, tl]
                return jnp.broadcast_to(row, (PEER_RC, LANES)).astype(BF16)
            cnt = [row16(n1_ref, h) for h in range(PEER_HEADS)]
            cf = [row16(a1_ref, h) for h in range(PEER_HEADS)]
            zero = jnp.zeros((PEER_RC, LANES), BF16)
            for r in range(PEER_KEYS // PEER_RC):
                j0 = r * PEER_RC
                w = None
                for h in range(PEER_HEADS):
                    jr = slice((h * PEER_KEYS + j0) // 2, (h * PEER_KEYS + j0 + PEER_RC) // 2)
                    r2 = pltpu.bitcast(r2_ref[jr, tl], BF16)
                    b2 = pltpu.bitcast(b2_ref[jr, tl], BF16)
                    wh = jnp.where(r2 < cnt[h], b2 * cf[h], zero)
                    w = wh if w is None else w + wh
                row0 = sub * PEER_KEYS + j0
                at[row0:row0 + PEER_RC, tl] = _gelu(ht[row0:row0 + PEER_RC, tl]).astype(BF16) * w

    scores(0)
    for g in range(n_grp):
        if g + 1 < n_grp:
            scores(g + 1)
        for sub in range(g * PEER_GRP // PEER_KEYS, (g + 1) * PEER_GRP // PEER_KEYS):
            gates(sub)
        if g % 2 == 1:
            rows = slice((g - 1) * PEER_GRP, (g + 1) * PEER_GRP)
            k, c0 = divmod((g - 1) * PEER_GRP, PEER_SLAB)
            acc[...] += _dot(vt_refs[k][:, c0:c0 + 2 * PEER_GRP], at[rows, :])

    @pl.when(e == pl.num_programs(1) - 1)
    def _():
        x3 = x2_ref[...] + acc[...].T
        o_ref[...] = _rms(x3, gf_ref[...])


def _peer_call(xpt, u16, vt16, r2, b2, n1, a1, x2, g_final, tb):
    d, t = xpt.shape
    n_exp = u16.shape[0]
    tok = lambda i, e: (i, 0)
    sel = pl.BlockSpec((PEER_HEADS * PEER_KEYS, tb), lambda i, e: (0, i))
    sel16 = pl.BlockSpec((PEER_HEADS * PEER_KEYS // 2, tb), lambda i, e: (0, i))
    u_specs = [pl.BlockSpec((PEER_SLAB, d), functools.partial(lambda i, e, k: (PEER_SPLIT * e + k, 0), k=k))
               for k in range(PEER_SPLIT)]
    vt_specs = [pl.BlockSpec((d, PEER_SLAB), functools.partial(lambda i, e, k: (0, PEER_SPLIT * e + k), k=k))
                for k in range(PEER_SPLIT)]
    return pl.pallas_call(
        _peer_kernel,
        grid=(t // tb, n_exp // PEER_EBLK),
        in_specs=[pl.BlockSpec((d, tb), lambda i, e: (0, i)),
                  *u_specs, *vt_specs,
                  sel16, sel16, sel, sel,
                  pl.BlockSpec((tb, d), tok),
                  pl.BlockSpec((1, d), lambda i, e: (0, 0))],
        out_specs=pl.BlockSpec((tb, d), tok),
        out_shape=jax.ShapeDtypeStruct((t, d), F32),
        scratch_shapes=[pltpu.VMEM((PEER_EBLK, tb), F32),
                        pltpu.VMEM((PEER_EBLK, tb), BF16),
                        pltpu.VMEM((d, tb), F32)],
        compiler_params=_params(("arbitrary", "arbitrary")),
        name="peer",
    )(xpt, *([u16] * PEER_SPLIT), *([vt16] * PEER_SPLIT), r2, b2, n1, a1, x2, g_final)


def _regroup_w_in(w_in):
    o_ba = 3 * SB_W + 4 * GDN_W
    ba = jnp.pad(w_in[:, o_ba:o_ba + 2 * GDN_HEADS], ((0, 0), (0, LANES - 2 * GDN_HEADS)))
    return jnp.concatenate([w_in[:, :o_ba], w_in[:, o_ba + 2 * GDN_HEADS:], ba], axis=1).astype(BF16)


def _layer(x2d, kv, batch, seq, n_mem, g_mix, w_in, gdn_conv, gdn_a_log, gdn_dt_bias, gdn_norm_g,
           w_sb_up, w_gdn_up, w_mix_out, g_cross, w_cq, w_co, g_peer, w_pq, peer_subkeys,
           peer_u, peer_v, g_out):
    t, d = x2d.shape
    tm = _pick(seq, (256, 128))
    sbqkv, gqkv, gz, gates, gba = _inproj_call(x2d, g_mix.reshape(1, d), _regroup_w_in(w_in), tm)
    o_sb = _sb_call(sbqkv, batch, seq)
    lane_pad = lambda v: jnp.pad(v.astype(F32), (GDN_HEADS, LANES - 2 * GDN_HEADS)).reshape(1, LANES)
    o_g = _gdn_call(gqkv, gz, gba, gdn_conv, lane_pad(gdn_a_log), lane_pad(gdn_dt_bias),
                    gdn_norm_g.reshape(1, GDN_HEAD_DIM), batch, seq, _pick(seq, (256, 128, 64)))
    x2, xpt = _merge_call(x2d, o_sb, o_g, gates, kv, w_sb_up.astype(BF16), w_gdn_up.astype(BF16),
                          w_mix_out.astype(BF16), g_cross.reshape(1, d), w_cq.astype(BF16),
                          w_co.astype(BF16), g_peer.reshape(1, d), seq, n_mem, tm)
    sk = peer_subkeys.reshape(N_HP, PEER_KEYS, -1).astype(BF16)
    r2, b2, n1, a1 = _peersel_call(xpt, w_pq.T.astype(BF16), sk, tm)
    return _peer_call(xpt, peer_u.astype(BF16), peer_v.T.astype(BF16), r2, b2, n1, a1, x2,
                      g_out.reshape(1, d), _pick(t, (512, 256, 128)))


def kernel(x, mem, g_mix, w_in, gdn_conv, gdn_a_log, gdn_dt_bias, gdn_norm_g, w_sb_up, w_gdn_up,
           w_mix_out, g_cross, g_mem, w_cq, w_ckv, w_co, g_peer, w_pq, peer_subkeys, peer_u, peer_v,
           g_final):
    batch, seq, d = x.shape
    n_mem = mem.shape[1]
    depth = g_mix.shape[0]
    assert depth == 1, "the final RMSNorm is fused into the last layer's PEER kernel"
    x2d = x.reshape(batch * seq, d)
    mem2d = mem.reshape(batch * n_mem, d)
    l = 0
    kv = _kv_call(mem2d, g_mem[l].reshape(1, d), w_ckv[l].astype(BF16), n_mem)
    out = _layer(x2d, kv, batch, seq, n_mem, g_mix[l], w_in[l], gdn_conv[l], gdn_a_log[l],
                 gdn_dt_bias[l], gdn_norm_g[l], w_sb_up[l], w_gdn_up[l], w_mix_out[l], g_cross[l],
                 w_cq[l], w_co[l], g_peer[l], w_pq[l], peer_subkeys[l], peer_u[l], peer_v[l], g_final)
    return out.reshape(batch, seq, d)
```
